```python
import jax, jax.numpy as jnp
from jax import lax
import numpy as np

D_MODEL = 1024
BATCH = 2
SEQ = 8192
DEPTH = 4

CHUNK = 64
EPS = 1e-5
CONV_WIDTH = D_MODEL
CONV_GROUPS = 16
SHORT_K = 3
SSD_HEAD_DIM = 64
SSD_HEADS = D_MODEL // SSD_HEAD_DIM
SSD_INNER = SSD_HEADS * SSD_HEAD_DIM
SSD_GROUPS = 2
SSD_STATE = 128
SSD_CONV_K = 4
SSD_CONV_DIM = SSD_INNER + 2 * SSD_GROUPS * SSD_STATE
MIX_WIDTH = CONV_WIDTH + SSD_INNER
D_FF = 4 * D_MODEL
IN_COLS = 3 * CONV_WIDTH + SSD_INNER + SSD_CONV_DIM + SSD_HEADS

kernel_name = "hybrid_shortconv_ssd_trunk"


def rmsnorm(x, w):
    xf = x.astype(jnp.float32)
    y = xf * lax.rsqrt(jnp.mean(xf * xf, axis=-1, keepdims=True) + EPS)
    return (y * w.astype(jnp.float32)).astype(x.dtype)


def causal_dwconv(u, w, b=None):
    k, c = w.shape
    y = lax.conv_general_dilated(
        u, w[:, None, :], window_strides=(1,), padding=[(k - 1, 0)],
        dimension_numbers=("NWC", "WIO", "NWC"), feature_group_count=c)
    if b is not None:
        y = y + b
    return y


def short_conv_mixer(u_b, u_c, u_h, conv_w):
    return u_b * causal_dwconv(u_c * u_h, conv_w)


def ssd_scan(xs, dt, a_head, bm, cm):
    f32 = jnp.float32
    b, t, h, p = xs.shape
    g, n = bm.shape[2], bm.shape[3]
    r = h // g
    nc = t // CHUNK
    x_c = (xs.astype(f32) * dt[..., None]).reshape(b, nc, CHUNK, g, r, p)
    a_c = (dt * a_head).reshape(b, nc, CHUNK, g, r)
    b_c = bm.astype(f32).reshape(b, nc, CHUNK, g, n)
    c_c = cm.astype(f32).reshape(b, nc, CHUNK, g, n)
    a_cum = jnp.cumsum(a_c, axis=2)
    causal = jnp.tril(jnp.ones((CHUNK, CHUNK), dtype=bool))
    seg = a_cum[:, :, :, None] - a_cum[:, :, None, :]
    decay = jnp.exp(jnp.where(causal[None, None, :, :, None, None], seg, -jnp.inf))
    scores = jnp.einsum("bclgn,bcsgn->bclsg", c_c, b_c)
    y_diag = jnp.einsum("bclsgr,bcsgrp->bclgrp", scores[..., None] * decay, x_c)
    decay_end = jnp.exp(a_cum[:, :, -1:] - a_cum)
    states = jnp.einsum("bclgn,bclgrp->bcgrpn", b_c, x_c * decay_end[..., None])
    chunk_decay = jnp.exp(a_cum[:, :, -1])

    def step(hs, inp):
        s_c, d_c = inp
        return hs * d_c[..., None, None] + s_c, hs

    h0 = jnp.zeros((b, g, r, p, n), dtype=f32)
    _, prev = lax.scan(step, h0, (jnp.moveaxis(states, 1, 0), jnp.moveaxis(chunk_decay, 1, 0)))
    prev = jnp.moveaxis(prev, 0, 1)
    y_off = jnp.einsum("bclgn,bcgrpn->bclgrp", c_c, prev) * jnp.exp(a_cum)[..., None]
    return (y_diag + y_off).reshape(b, t, h, p)


def ssd_mixer(z, xbc, dt_raw, conv_w, conv_b, dt_bias, a_log, d_skip, norm_w):
    b, t, _ = z.shape
    xbc = jax.nn.silu(causal_dwconv(xbc, conv_w, conv_b))
    xs, bm, cm = jnp.split(xbc, [SSD_INNER, SSD_INNER + SSD_GROUPS * SSD_STATE], axis=-1)
    xs = xs.reshape(b, t, SSD_HEADS, SSD_HEAD_DIM)
    bm = bm.reshape(b, t, SSD_GROUPS, SSD_STATE)
    cm = cm.reshape(b, t, SSD_GROUPS, SSD_STATE)
    dt = jax.nn.softplus(dt_raw.astype(jnp.float32) + dt_bias.astype(jnp.float32))
    a_head = -jnp.exp(a_log.astype(jnp.float32))
    y = ssd_scan(xs, dt, a_head, bm, cm)
    y = y + d_skip.astype(jnp.float32)[:, None] * xs.astype(jnp.float32)
    y = y.reshape(b, t, SSD_INNER).astype(z.dtype)
    gated = (y * jax.nn.silu(z)).reshape(b, t, SSD_GROUPS, SSD_INNER // SSD_GROUPS)
    gated = rmsnorm(gated, norm_w.reshape(SSD_GROUPS, SSD_INNER // SSD_GROUPS))
    return gated.reshape(b, t, SSD_INNER)


SPLITS = list(np.cumsum([CONV_WIDTH, CONV_WIDTH, CONV_WIDTH, SSD_INNER, SSD_CONV_DIM]))


def hybrid_layer(x, norm_mix_w, w_in, short_conv_w, ssd_conv_w, ssd_conv_b, dt_bias,
                 a_log, d_skip, ssd_norm_w, w_out, norm_mlp_w, w_up, w_down):
    h = rmsnorm(x, norm_mix_w)
    proj = jnp.einsum("btd,dc->btc", h, w_in)
    u_b, u_c, u_h, z, xbc, dt_raw = jnp.split(proj, SPLITS, axis=-1)
    y_a = short_conv_mixer(u_b, u_c, u_h, short_conv_w)
    y_b = ssd_mixer(z, xbc, dt_raw, ssd_conv_w, ssd_conv_b, dt_bias, a_log, d_skip, ssd_norm_w)
    y = jnp.concatenate([y_a, y_b], axis=-1)
    x = x + jnp.einsum("btc,cd->btd", y, w_out)
    h = rmsnorm(x, norm_mlp_w)
    hid = jnp.square(jax.nn.relu(jnp.einsum("btd,df->btf", h, w_up)))
    return x + jnp.einsum("btf,fd->btd", hid, w_down)


def setup_inputs(seed: int = 0) -> dict:
    key = jax.random.key(seed)
    ks = jax.random.split(key, 16)
    f32 = jnp.float32
    nrm = lambda k, shape, s: jax.random.normal(k, shape, f32) * s
    gain = lambda k, shape: 1.0 + 0.02 * jax.random.normal(k, shape, f32)
    dt0 = jnp.exp(jax.random.uniform(ks[6], (DEPTH, SSD_HEADS), f32, math_log(1e-3), math_log(1e-1)))
    dt_bias = dt0 + jnp.log(-jnp.expm1(-dt0))
    return {
        "x": jax.random.normal(ks[0], (BATCH, SEQ, D_MODEL), f32),
        "norm_mix_w": gain(ks[1], (DEPTH, D_MODEL)),
        "w_in": nrm(ks[2], (DEPTH, D_MODEL, IN_COLS), D_MODEL ** -0.5),
        "short_conv_w": nrm(ks[3], (DEPTH, SHORT_K, CONV_WIDTH), SHORT_K ** -0.5),
        "ssd_conv_w": nrm(ks[4], (DEPTH, SSD_CONV_K, SSD_CONV_DIM), SSD_CONV_K ** -0.5),
        "ssd_conv_b": nrm(ks[5], (DEPTH, SSD_CONV_DIM), 0.02),
        "dt_bias": dt_bias,
        "a_log": jnp.log(jax.random.uniform(ks[7], (DEPTH, SSD_HEADS), f32, 1.0, 16.0)),
        "d_skip": gain(ks[8], (DEPTH, SSD_HEADS)),
        "ssd_norm_w": gain(ks[9], (DEPTH, SSD_INNER)),
        "w_out": nrm(ks[10], (DEPTH, MIX_WIDTH, D_MODEL), MIX_WIDTH ** -0.5),
        "norm_mlp_w": gain(ks[11], (DEPTH, D_MODEL)),
        "w_up": nrm(ks[12], (DEPTH, D_MODEL, D_FF), D_MODEL ** -0.5),
        "w_down": nrm(ks[13], (DEPTH, D_FF, D_MODEL), D_FF ** -0.5),
        "final_norm_w": gain(ks[14], (D_MODEL,)),
    }


def math_log(v):
    return float(np.log(v))


def reference(x, norm_mix_w, w_in, short_conv_w, ssd_conv_w, ssd_conv_b, dt_bias, a_log,
              d_skip, ssd_norm_w, w_out, norm_mlp_w, w_up, w_down, final_norm_w):
    for i in range(DEPTH):
        x = hybrid_layer(x, norm_mix_w[i], w_in[i], short_conv_w[i], ssd_conv_w[i],
                         ssd_conv_b[i], dt_bias[i], a_log[i], d_skip[i], ssd_norm_w[i],
                         w_out[i], norm_mlp_w[i], w_up[i], w_down[i])
    return rmsnorm(x, final_norm_w)
```

```python
import functools

import jax
import jax.numpy as jnp
from jax import lax
from jax.experimental import pallas as pl
from jax.experimental.pallas import tpu as pltpu

D_MODEL = 1024
CHUNK = 64
EPS = 1e-5
CONV_WIDTH = 1024
SHORT_K = 3
SSD_HEAD_DIM = 64
SSD_HEADS = 16
SSD_INNER = 1024
SSD_GROUPS = 2
SSD_STATE = 128
SSD_CONV_K = 4
SSD_CONV_DIM = SSD_INNER + 2 * SSD_GROUPS * SSD_STATE
MIX_WIDTH = CONV_WIDTH + SSD_INNER
D_FF = 4 * D_MODEL
MAIN_COLS = 3 * CONV_WIDTH + SSD_INNER + SSD_CONV_DIM
COL_UB, COL_UC, COL_UH, COL_Z, COL_XBC = 0, 1024, 2048, 3072, 4096
XBC_B = SSD_INNER
XBC_C = SSD_INNER + SSD_GROUPS * SSD_STATE
LANES = 128
HIST = 8
PAIRS = SSD_HEADS // 2
PAIRS_PER_GROUP = PAIRS // SSD_GROUPS
GROUP_COLS = SSD_INNER // SSD_GROUPS

F32 = jnp.float32
BF16 = jnp.bfloat16


def _rms(x, w):
    ms = jnp.mean(x * x, axis=-1, keepdims=True)
    return x * lax.rsqrt(ms + EPS) * w


def _softplus(x):
    return jnp.maximum(x, 0.0) + jnp.log1p(jnp.exp(-jnp.abs(x)))


def _silu(x):
    return x * jax.nn.sigmoid(x)


def _mixer_kernel(x_ref, nw_ref, wmain_ref, wdt_ref, scw_ref, ssw_ref, ssb_ref, dtb_ref, alog_ref,
                  dskip_ref, gnw_ref, wout_ref, o_ref,
                  h_ref, pj_ref, vext_ref, xbcext_ref, xbc_ref, dt_ref, a_ref, ybuf_ref, ymix_ref, s_ref,
                  *, tt, nc, rb):
    t = pl.program_id(1)

    @pl.when(t == 0)
    def _():
        vext_ref[0:HIST, :] = jnp.zeros((HIST, CONV_WIDTH), F32)
        xbcext_ref[0:HIST, :] = jnp.zeros((HIST, SSD_CONV_DIM), F32)
        s_ref[...] = jnp.zeros(s_ref.shape, F32)

    for r0 in range(0, tt, rb):
        h_ref[r0:r0 + rb, :] = _rms(x_ref[r0:r0 + rb, :], nw_ref[...]).astype(BF16)

    for c0 in range(0, COL_XBC, nc):
        pj_ref[:, c0:c0 + nc] = jnp.dot(h_ref[...], wmain_ref[:, c0:c0 + nc], preferred_element_type=F32)
    for c0 in range(0, SSD_CONV_DIM, nc):
        xbcext_ref[HIST:HIST + tt, c0:c0 + nc] = jnp.dot(
            h_ref[...], wmain_ref[:, COL_XBC + c0:COL_XBC + c0 + nc], preferred_element_type=F32)
    dt = _softplus(jnp.dot(h_ref[...], wdt_ref[...], preferred_element_type=F32) + dtb_ref[...])
    dt_ref[...] = dt
    a_ref[...] = dt * (-jnp.exp(alog_ref[...]))

    for r0 in range(0, tt, rb):
        vext_ref[HIST + r0:HIST + r0 + rb, :] = (
            pj_ref[r0:r0 + rb, COL_UC:COL_UC + CONV_WIDTH] * pj_ref[r0:r0 + rb, COL_UH:COL_UH + CONV_WIDTH])
    for r0 in range(0, tt, rb):
        acc = scw_ref[SHORT_K - 1:SHORT_K, :] * vext_ref[HIST + r0:HIST + r0 + rb, :]
        for k in range(1, SHORT_K):
            acc = acc + scw_ref[SHORT_K - 1 - k:SHORT_K - k, :] * vext_ref[HIST + r0 - k:HIST + r0 - k + rb, :]
        ymix_ref[r0:r0 + rb, 0:CONV_WIDTH] = (pj_ref[r0:r0 + rb, COL_UB:COL_UB + CONV_WIDTH] * acc).astype(BF16)
    vext_ref[0:HIST, :] = vext_ref[tt:tt + HIST, :]

    for r0 in range(0, tt, rb):
        acc = ssb_ref[...] + ssw_ref[SSD_CONV_K - 1:SSD_CONV_K, :] * xbcext_ref[HIST + r0:HIST + r0 + rb, :]
        for k in range(1, SSD_CONV_K):
            acc = acc + (ssw_ref[SSD_CONV_K - 1 - k:SSD_CONV_K - k, :]
                         * xbcext_ref[HIST + r0 - k:HIST + r0 - k + rb, :])
        xbc_ref[r0:r0 + rb, :] = _silu(acc)
    xbcext_ref[0:HIST, :] = xbcext_ref[tt:tt + HIST, :]

    row_i = lax.broadcasted_iota(jnp.int32, (2 * CHUNK, LANES), 0)
    col_i = lax.broadcasted_iota(jnp.int32, (2 * CHUNK, LANES), 1)
    same_half = (row_i >= CHUNK) == (col_i >= CHUNK)
    tril2 = jnp.where(same_half & (col_i <= row_i), 1.0, 0.0).astype(F32)
    sub_i = lax.broadcasted_iota(jnp.int32, (CHUNK, LANES), 0)
    lane_i = lax.broadcasted_iota(jnp.int32, (CHUNK, LANES), 1)
    causal2 = (lane_i & (CHUNK - 1)) <= sub_i
    left_half = lane_i < CHUNK

    def pair_cols(zmat, j):
        even = jnp.broadcast_to(zmat[0:CHUNK, j:j + 1], (CHUNK, LANES))
        odd = jnp.broadcast_to(zmat[CHUNK:2 * CHUNK, j:j + 1], (CHUNK, LANES))
        return jnp.where(left_half, even, odd)

    def chunk_body(c, carry):
        r0 = pl.multiple_of(c * CHUNK, CHUNK)
        rows = pl.ds(r0, CHUNK)
        a_z = jnp.concatenate([a_ref[rows, 0:LANES], a_ref[rows, LANES:2 * LANES]], axis=0)
        dt_z = jnp.concatenate([dt_ref[rows, 0:LANES], dt_ref[rows, LANES:2 * LANES]], axis=0)
        acum_z = jnp.dot(tril2, a_z, precision=lax.Precision.HIGHEST, preferred_element_type=F32)
        acum_r = acum_z.T
        dt_r = dt_z.T
        for g in range(SSD_GROUPS):
            c_g = xbc_ref[rows, XBC_C + g * SSD_STATE:XBC_C + (g + 1) * SSD_STATE]
            b_g = xbc_ref[rows, XBC_B + g * SSD_STATE:XBC_B + (g + 1) * SSD_STATE]
            c_bf = c_g.astype(BF16)
            b2 = jnp.concatenate([b_g, b_g], axis=0)
            g2 = lax.dot_general(c_bf, b2.astype(BF16), (((1,), (1,)), ((), ())),
                                 preferred_element_type=F32)
            b_t = b2.T[:, 0:CHUNK].astype(BF16)
            s_g = s_ref[g]
            cs = jnp.dot(c_bf, s_g.astype(BF16), preferred_element_type=F32)
            for jj in range(PAIRS_PER_GROUP):
                j = g * PAIRS_PER_GROUP + jj
                cols = slice(j * LANES, (j + 1) * LANES)
                gcols = slice(jj * LANES, (jj + 1) * LANES)
                col_a = pair_cols(acum_z, j)
                col_dt = pair_cols(dt_z, j)
                decay = jnp.exp(jnp.where(causal2, col_a - acum_r[j:j + 1, :], -jnp.inf))
                m2 = (g2 * decay * dt_r[j:j + 1, :]).astype(BF16)
                xs2 = xbc_ref[rows, cols]
                xx = jnp.where(same_half, jnp.concatenate([xs2, xs2], axis=0), 0.0).astype(BF16)
                y = jnp.dot(m2, xx, preferred_element_type=F32)
                y = y + cs[:, gcols] * jnp.exp(col_a)
                y = y + dskip_ref[:, cols] * xs2
                z2 = pj_ref[rows, COL_Z + j * LANES:COL_Z + (j + 1) * LANES]
                ybuf_ref[rows, cols] = y * _silu(z2)
                last_a = col_a[CHUNK - 1:CHUNK, :]
                xd = (xs2 * col_dt * jnp.exp(last_a - col_a)).astype(BF16)
                ds = jnp.dot(b_t, xd, preferred_element_type=F32)
                s_ref[g, :, gcols] = s_g[:, gcols] * jnp.exp(last_a) + ds
        return carry

    lax.fori_loop(0, tt // CHUNK, chunk_body, 0)

    for r0 in range(0, tt, rb):
        for g in range(SSD_GROUPS):
            gc = slice(g * GROUP_COLS, (g + 1) * GROUP_COLS)
            yg = _rms(ybuf_ref[r0:r0 + rb, gc], gnw_ref[:, gc])
            ymix_ref[r0:r0 + rb, CONV_WIDTH + g * GROUP_COLS:CONV_WIDTH + (g + 1) * GROUP_COLS] = yg.astype(BF16)

    for n0 in range(0, D_MODEL, nc):
        o_ref[:, n0:n0 + nc] = x_ref[:, n0:n0 + nc] + jnp.dot(
            ymix_ref[...], wout_ref[:, n0:n0 + nc], preferred_element_type=F32)


def _mlp_kernel(x_ref, nw_ref, wup_ref, wdn_ref, fw_ref, o_ref, h_ref, hid_ref, *, tm, nc, rb, final_norm):
    for r0 in range(0, tm, rb):
        h_ref[r0:r0 + rb, :] = _rms(x_ref[r0:r0 + rb, :], nw_ref[...]).astype(BF16)
    for f0 in range(0, D_FF, nc):
        u = jnp.maximum(jnp.dot(h_ref[...], wup_ref[:, f0:f0 + nc], preferred_element_type=F32), 0.0)
        hid_ref[:, f0:f0 + nc] = (u * u).astype(BF16)
    for n0 in range(0, D_MODEL, nc):
        o_ref[:, n0:n0 + nc] = x_ref[:, n0:n0 + nc] + jnp.dot(
            hid_ref[...], wdn_ref[:, n0:n0 + nc], preferred_element_type=F32)
    if final_norm:
        for r0 in range(0, tm, rb):
            o_ref[r0:r0 + rb, :] = _rms(o_ref[r0:r0 + rb, :], fw_ref[...])


def _resident(shape, layer):
    nd = len(shape)
    return pl.BlockSpec((None,) + tuple(shape), lambda *_: (layer,) + (0,) * nd,
                        pipeline_mode=pl.Buffered(1))


def _mixer_call(x, p, layer, tt, vmem_limit):
    b, t, d = x.shape
    kern = functools.partial(_mixer_kernel, tt=tt, nc=512, rb=32)
    in_specs = [
        pl.BlockSpec((None, tt, d), lambda bi, ti: (bi, ti, 0)),
        _resident((1, d), layer),
        _resident((d, MAIN_COLS), layer),
        _resident((d, 2 * LANES), layer),
        _resident((HIST, CONV_WIDTH), layer),
        _resident((HIST, SSD_CONV_DIM), layer),
        _resident((1, SSD_CONV_DIM), layer),
        _resident((1, 2 * LANES), layer),
        _resident((1, 2 * LANES), layer),
        _resident((1, SSD_INNER), layer),
        _resident((1, SSD_INNER), layer),
        _resident((MIX_WIDTH, d), layer),
    ]
    scratch = [
        pltpu.VMEM((tt, d), BF16),
        pltpu.VMEM((tt, COL_XBC), F32),
        pltpu.VMEM((tt + HIST, CONV_WIDTH), F32),
        pltpu.VMEM((tt + HIST, SSD_CONV_DIM), F32),
        pltpu.VMEM((tt, SSD_CONV_DIM), F32),
        pltpu.VMEM((tt, 2 * LANES), F32),
        pltpu.VMEM((tt, 2 * LANES), F32),
        pltpu.VMEM((tt, SSD_INNER), F32),
        pltpu.VMEM((tt, MIX_WIDTH), BF16),
        pltpu.VMEM((SSD_GROUPS, SSD_STATE, GROUP_COLS), F32),
    ]
    return pl.pallas_call(
        kern,
        out_shape=jax.ShapeDtypeStruct(x.shape, x.dtype),
        grid=(b, t // tt),
        in_specs=in_specs,
        out_specs=pl.BlockSpec((None, tt, d), lambda bi, ti: (bi, ti, 0)),
        scratch_shapes=scratch,
        compiler_params=pltpu.CompilerParams(
            dimension_semantics=("arbitrary", "arbitrary"), vmem_limit_bytes=vmem_limit),
        name=f"mixer_l{layer}",
    )(x, p["norm_mix_w"], p["w_main"], p["w_dt"], p["short_conv_w"], p["ssd_conv_w"], p["ssd_conv_b"],
      p["dt_bias"], p["a_log"], p["d_skip"], p["ssd_norm_w"], p["w_out"])


def _mlp_call(x2, p, layer, tm, vmem_limit, final_norm):
    n, d = x2.shape
    kern = functools.partial(_mlp_kernel, tm=tm, nc=512, rb=32, final_norm=final_norm)
    return pl.pallas_call(
        kern,
        out_shape=jax.ShapeDtypeStruct(x2.shape, x2.dtype),
        grid=(n // tm,),
        in_specs=[
            pl.BlockSpec((tm, d), lambda i: (i, 0)),
            _resident((1, d), layer),
            _resident((d, D_FF), layer),
            _resident((D_FF, d), layer),
            pl.BlockSpec((1, d), lambda i: (0, 0)),
        ],
        out_specs=pl.BlockSpec((tm, d), lambda i: (i, 0)),
        scratch_shapes=[pltpu.VMEM((tm, d), BF16), pltpu.VMEM((tm, D_FF), BF16)],
        compiler_params=pltpu.CompilerParams(
            dimension_semantics=("arbitrary",), vmem_limit_bytes=vmem_limit),
        name=f"mlp_l{layer}",
    )(x2, p["norm_mlp_w"], p["w_up"], p["w_down"], p["final_norm_w"])


def _even_odd_lanes(v):
    pad = [(0, 0)] * (v.ndim - 1) + [(0, LANES - PAIRS)]
    return jnp.concatenate([jnp.pad(v[..., 0::2], pad), jnp.pad(v[..., 1::2], pad)], axis=-1)


def _pad_rows(w, rows):
    return jnp.pad(w, ((0, 0), (0, rows - w.shape[1]), (0, 0)))


def kernel(x, norm_mix_w, w_in, short_conv_w, ssd_conv_w, ssd_conv_b, dt_bias, a_log, d_skip, ssd_norm_w,
           w_out, norm_mlp_w, w_up, w_down, final_norm_w):
    depth = w_in.shape[0]
    b, t, d = x.shape
    tt = min(256, t)
    tm = min(512, b * t)
    assert d == D_MODEL and t % tt == 0 and tt % CHUNK == 0 and (b * t) % tm == 0
    vmem_limit = 56 * 1024 * 1024
    p = {
        "norm_mix_w": norm_mix_w[:, None, :],
        "w_main": w_in[:, :, :MAIN_COLS].astype(BF16),
        "w_dt": _even_odd_lanes(w_in[:, :, MAIN_COLS:]).astype(BF16),
        "short_conv_w": _pad_rows(short_conv_w, HIST),
        "ssd_conv_w": _pad_rows(ssd_conv_w, HIST),
        "ssd_conv_b": ssd_conv_b[:, None, :],
        "dt_bias": _even_odd_lanes(dt_bias)[:, None, :],
        "a_log": _even_odd_lanes(a_log)[:, None, :],
        "d_skip": jnp.repeat(d_skip, SSD_HEAD_DIM, axis=-1)[:, None, :],
        "ssd_norm_w": ssd_norm_w[:, None, :],
        "w_out": w_out.astype(BF16),
        "norm_mlp_w": norm_mlp_w[:, None, :],
        "w_up": w_up.astype(BF16),
        "w_down": w_down.astype(BF16),
        "final_norm_w": final_norm_w[None, :],
    }
    for layer in range(depth):
        x = _mixer_call(x, p, layer, tt, vmem_limit)
        x = _mlp_call(x.reshape(b * t, d), p, layer, tm, vmem_limit, layer == depth - 1).reshape(b, t, d)
    return x
```

```python
import functools

import jax
import jax.numpy as jnp
from jax import lax
from jax.experimental import pallas as pl
from jax.experimental.pallas import tpu as pltpu

D_MODEL = 1024
CHUNK = 64
EPS = 1e-5
CONV_WIDTH = 1024
SHORT_K = 3
SSD_HEAD_DIM = 64
SSD_HEADS = 16
SSD_INNER = 1024
SSD_GROUPS = 2
SSD_STATE = 128
SSD_CONV_K = 4
SSD_CONV_DIM = SSD_INNER + 2 * SSD_GROUPS * SSD_STATE
MIX_WIDTH = CONV_WIDTH + SSD_INNER
D_FF = 4 * D_MODEL
MAIN_COLS = 3 * CONV_WIDTH + SSD_INNER + SSD_CONV_DIM
COL_UB, COL_UC, COL_UH, COL_Z, COL_XBC = 0, 1024, 2048, 3072, 4096
XBC_B = SSD_INNER
XBC_C = SSD_INNER + SSD_GROUPS * SSD_STATE
LANES = 128
HIST = 8
PAIRS = SSD_HEADS // 2
PAIRS_PER_GROUP = PAIRS // SSD_GROUPS
GROUP_COLS = SSD_INNER // SSD_GROUPS

F32 = jnp.float32
BF16 = jnp.bfloat16


def _rms(x, w):
    ms = jnp.mean(x * x, axis=-1, keepdims=True)
    return x * lax.rsqrt(ms + EPS) * w


def _softplus(x):
    return jnp.maximum(x, 0.0) + jnp.log1p(jnp.exp(-jnp.abs(x)))


def _silu(x):
    return x * jax.nn.sigmoid(x)


def _mixer_kernel(x_ref, xn_ref, nw_ref, wmain_ref, wdt_ref, scw_ref, ssw_ref, ssb_ref, dtb_ref, alog_ref,
                  dskip_ref, gnw_ref, wout_ref, o_ref,
                  h_ref, pj_ref, vext_ref, xbcext_ref, dtraw_ref, xbc_ref, zs_ref, dt_ref, a_ref, ybuf_ref,
                  ymix_ref, s_ref, *, tt, nc, rb, tiles_per_seq):
    g = pl.program_id(0)

    def norm_rows(src_ref, r0):
        h_ref[r0:r0 + rb, :] = _rms(src_ref[r0:r0 + rb, :], nw_ref[...]).astype(BF16)

    def project_cols(c0):
        if c0 < COL_XBC:
            pj_ref[:, c0:c0 + nc] = jnp.dot(h_ref[...], wmain_ref[:, c0:c0 + nc], preferred_element_type=F32)
        elif c0 < MAIN_COLS:
            xbcext_ref[HIST:HIST + tt, c0 - COL_XBC:c0 - COL_XBC + nc] = jnp.dot(
                h_ref[...], wmain_ref[:, c0:c0 + nc], preferred_element_type=F32)
        else:
            dtraw_ref[...] = jnp.dot(h_ref[...], wdt_ref[...], preferred_element_type=F32)

    proj_blocks = list(range(0, MAIN_COLS, nc)) + [MAIN_COLS]
    row_blocks = list(range(0, tt, rb))

    @pl.when(g == 0)
    def _():
        for r0 in row_blocks:
            norm_rows(x_ref, r0)
        for c0 in proj_blocks:
            project_cols(c0)

    @pl.when(g % tiles_per_seq == 0)
    def _():
        vext_ref[0:HIST, :] = jnp.zeros((HIST, CONV_WIDTH), F32)
        xbcext_ref[0:HIST, :] = jnp.zeros((HIST, SSD_CONV_DIM), F32)
        s_ref[...] = jnp.zeros(s_ref.shape, F32)

    def short_conv_rows(r0):
        acc = scw_ref[SHORT_K - 1:SHORT_K, :] * vext_ref[HIST + r0:HIST + r0 + rb, :]
        for k in range(1, SHORT_K):
            acc = acc + scw_ref[SHORT_K - 1 - k:SHORT_K - k, :] * vext_ref[HIST + r0 - k:HIST + r0 - k + rb, :]
        ymix_ref[r0:r0 + rb, 0:CONV_WIDTH] = (pj_ref[r0:r0 + rb, COL_UB:COL_UB + CONV_WIDTH] * acc).astype(BF16)

    def ssd_conv_rows(r0):
        acc = ssb_ref[...] + ssw_ref[SSD_CONV_K - 1:SSD_CONV_K, :] * xbcext_ref[HIST + r0:HIST + r0 + rb, :]
        for k in range(1, SSD_CONV_K):
            acc = acc + (ssw_ref[SSD_CONV_K - 1 - k:SSD_CONV_K - k, :]
                         * xbcext_ref[HIST + r0 - k:HIST + r0 - k + rb, :])
        xbc_ref[r0:r0 + rb, :] = _silu(acc)

    row_i = lax.broadcasted_iota(jnp.int32, (2 * CHUNK, LANES), 0)
    col_i = lax.broadcasted_iota(jnp.int32, (2 * CHUNK, LANES), 1)
    same_half = (row_i >= CHUNK) == (col_i >= CHUNK)
    tril2 = jnp.where(same_half & (col_i <= row_i), 1.0, 0.0).astype(F32)
    sub_i = lax.broadcasted_iota(jnp.int32, (CHUNK, LANES), 0)
    lane_i = lax.broadcasted_iota(jnp.int32, (CHUNK, LANES), 1)
    causal2 = (lane_i & (CHUNK - 1)) <= sub_i
    left_half = lane_i < CHUNK

    def pair_cols(zmat, j):
        even = jnp.broadcast_to(zmat[0:CHUNK, j:j + 1], (CHUNK, LANES))
        odd = jnp.broadcast_to(zmat[CHUNK:2 * CHUNK, j:j + 1], (CHUNK, LANES))
        return jnp.where(left_half, even, odd)

    def chunk_prep(c):
        rows = slice(c * CHUNK, (c + 1) * CHUNK)
        a_z = jnp.concatenate([a_ref[rows, 0:LANES], a_ref[rows, LANES:2 * LANES]], axis=0)
        dt_z = jnp.concatenate([dt_ref[rows, 0:LANES], dt_ref[rows, LANES:2 * LANES]], axis=0)
        acum_z = jnp.dot(tril2, a_z, precision=lax.Precision.HIGHEST, preferred_element_type=F32)
        return rows, acum_z, dt_z, acum_z.T, dt_z.T

    def chunk_group(prep, grp):
        rows, acum_z, dt_z, acum_r, dt_r = prep
        c_g = xbc_ref[rows, XBC_C + grp * SSD_STATE:XBC_C + (grp + 1) * SSD_STATE]
        b_g = xbc_ref[rows, XBC_B + grp * SSD_STATE:XBC_B + (grp + 1) * SSD_STATE]
        c_bf = c_g.astype(BF16)
        b2 = jnp.concatenate([b_g, b_g], axis=0)
        g2 = lax.dot_general(c_bf, b2.astype(BF16), (((1,), (1,)), ((), ())),
                             preferred_element_type=F32)
        b_t = b2.T[:, 0:CHUNK].astype(BF16)
        s_g = s_ref[grp]
        cs = jnp.dot(c_bf, s_g.astype(BF16), preferred_element_type=F32)
        for jj in range(PAIRS_PER_GROUP):
            j = grp * PAIRS_PER_GROUP + jj
            cols = slice(j * LANES, (j + 1) * LANES)
            gcols = slice(jj * LANES, (jj + 1) * LANES)
            col_a = pair_cols(acum_z, j)
            col_dt = pair_cols(dt_z, j)
            decay = jnp.exp(jnp.where(causal2, col_a - acum_r[j:j + 1, :], -jnp.inf))
            m2 = (g2 * decay * dt_r[j:j + 1, :]).astype(BF16)
            xs2 = xbc_ref[rows, cols]
            xx = jnp.where(same_half, jnp.concatenate([xs2, xs2], axis=0), 0.0).astype(BF16)
            y = jnp.dot(m2, xx, preferred_element_type=F32)
            y = y + cs[:, gcols] * jnp.exp(col_a)
            y = y + dskip_ref[:, cols] * xs2
            ybuf_ref[rows, cols] = y * zs_ref[rows, cols]
            last_a = col_a[CHUNK - 1:CHUNK, :]
            xd = (xs2 * col_dt * jnp.exp(last_a - col_a)).astype(BF16)
            ds = jnp.dot(b_t, xd, preferred_element_type=F32)
            s_ref[grp, :, gcols] = s_g[:, gcols] * jnp.exp(last_a) + ds

    def group_norm_rows(r0):
        for grp in range(SSD_GROUPS):
            gc = slice(grp * GROUP_COLS, (grp + 1) * GROUP_COLS)
            yg = _rms(ybuf_ref[r0:r0 + rb, gc], gnw_ref[:, gc])
            ymix_ref[r0:r0 + rb, CONV_WIDTH + grp * GROUP_COLS:CONV_WIDTH + (grp + 1) * GROUP_COLS] = (
                yg.astype(BF16))

    pending = list(proj_blocks)

    def project_next(n):
        for _ in range(n):
            if pending:
                project_cols(pending.pop(0))

    for r0 in row_blocks:
        norm_rows(xn_ref, r0)
    for r0 in row_blocks:
        vext_ref[HIST + r0:HIST + r0 + rb, :] = (
            pj_ref[r0:r0 + rb, COL_UC:COL_UC + CONV_WIDTH] * pj_ref[r0:r0 + rb, COL_UH:COL_UH + CONV_WIDTH])
    for r0 in row_blocks:
        short_conv_rows(r0)
    vext_ref[0:HIST, :] = vext_ref[tt:tt + HIST, :]
    project_next(1)
    for r0 in row_blocks:
        zs_ref[r0:r0 + rb, :] = _silu(pj_ref[r0:r0 + rb, COL_Z:COL_Z + SSD_INNER])
    project_next(1)
    half = len(row_blocks) // 2
    for r0 in row_blocks[:half]:
        ssd_conv_rows(r0)
    project_next(1)
    for r0 in row_blocks[half:]:
        ssd_conv_rows(r0)
    xbcext_ref[0:HIST, :] = xbcext_ref[tt:tt + HIST, :]
    dt = _softplus(dtraw_ref[...] + dtb_ref[...])
    dt_ref[...] = dt
    a_ref[...] = dt * (-jnp.exp(alog_ref[...]))
    project_next(1)
    n_chunks = tt // CHUNK
    per_group = -(-len(pending) // (n_chunks * SSD_GROUPS))
    for c in range(n_chunks):
        prep = chunk_prep(c)
        for grp in range(SSD_GROUPS):
            project_next(per_group)
            chunk_group(prep, grp)
    project_next(len(pending))
    for r0 in row_blocks:
        group_norm_rows(r0)
    for n0 in range(0, D_MODEL, nc):
        o_ref[:, n0:n0 + nc] = x_ref[:, n0:n0 + nc] + jnp.dot(
            ymix_ref[...], wout_ref[:, n0:n0 + nc], preferred_element_type=F32)


def _mlp_kernel(x_ref, nw_ref, wup_ref, wdn_ref, fw_ref, o_ref, h_ref, hid_ref, *, tm, nc, rb, final_norm):
    for r0 in range(0, tm, rb):
        h_ref[r0:r0 + rb, :] = _rms(x_ref[r0:r0 + rb, :], nw_ref[...]).astype(BF16)
    for f0 in range(0, D_FF, nc):
        u = jnp.maximum(jnp.dot(h_ref[...], wup_ref[:, f0:f0 + nc], preferred_element_type=F32), 0.0)
        hid_ref[:, f0:f0 + nc] = (u * u).astype(BF16)
    for n0 in range(0, D_MODEL, nc):
        o_ref[:, n0:n0 + nc] = x_ref[:, n0:n0 + nc] + jnp.dot(
            hid_ref[...], wdn_ref[:, n0:n0 + nc], preferred_element_type=F32)
    if final_norm:
        for r0 in range(0, tm, rb):
            o_ref[r0:r0 + rb, :] = _rms(o_ref[r0:r0 + rb, :], fw_ref[...])


def _resident(shape, layer):
    nd = len(shape)
    return pl.BlockSpec((None,) + tuple(shape), lambda *_: (layer,) + (0,) * nd,
                        pipeline_mode=pl.Buffered(1))


def _mixer_call(x2, p, layer, tt, tiles_per_seq, vmem_limit):
    n, d = x2.shape
    n_tiles = n // tt
    kern = functools.partial(_mixer_kernel, tt=tt, nc=512, rb=32, tiles_per_seq=tiles_per_seq)
    in_specs = [
        pl.BlockSpec((tt, d), lambda g: (g, 0)),
        pl.BlockSpec((tt, d), lambda g: (jnp.minimum(g + 1, n_tiles - 1), 0)),
        _resident((1, d), layer),
        _resident((d, MAIN_COLS), layer),
        _resident((d, 2 * LANES), layer),
        _resident((HIST, CONV_WIDTH), layer),
        _resident((HIST, SSD_CONV_DIM), layer),
        _resident((1, SSD_CONV_DIM), layer),
        _resident((1, 2 * LANES), layer),
        _resident((1, 2 * LANES), layer),
        _resident((1, SSD_INNER), layer),
        _resident((1, SSD_INNER), layer),
        _resident((MIX_WIDTH, d), layer),
    ]
    scratch = [
        pltpu.VMEM((tt, d), BF16),
        pltpu.VMEM((tt, COL_XBC), F32),
        pltpu.VMEM((tt + HIST, CONV_WIDTH), F32),
        pltpu.VMEM((tt + HIST, SSD_CONV_DIM), F32),
        pltpu.VMEM((tt, 2 * LANES), F32),
        pltpu.VMEM((tt, SSD_CONV_DIM), F32),
        pltpu.VMEM((tt, SSD_INNER), F32),
        pltpu.VMEM((tt, 2 * LANES), F32),
        pltpu.VMEM((tt, 2 * LANES), F32),
        pltpu.VMEM((tt, SSD_INNER), F32),
        pltpu.VMEM((tt, MIX_WIDTH), BF16),
        pltpu.VMEM((SSD_GROUPS, SSD_STATE, GROUP_COLS), F32),
    ]
    return pl.pallas_call(
        kern,
        out_shape=jax.ShapeDtypeStruct(x2.shape, x2.dtype),
        grid=(n_tiles,),
        in_specs=in_specs,
        out_specs=pl.BlockSpec((tt, d), lambda g: (g, 0)),
        scratch_shapes=scratch,
        compiler_params=pltpu.CompilerParams(
            dimension_semantics=("arbitrary",), vmem_limit_bytes=vmem_limit),
        name=f"mixer_l{layer}",
    )(x2, x2, p["norm_mix_w"], p["w_main"], p["w_dt"], p["short_conv_w"], p["ssd_conv_w"], p["ssd_conv_b"],
      p["dt_bias"], p["a_log"], p["d_skip"], p["ssd_norm_w"], p["w_out"])


def _mlp_call(x2, p, layer, tm, vmem_limit, final_norm):
    n, d = x2.shape
    kern = functools.partial(_mlp_kernel, tm=tm, nc=512, rb=32, final_norm=final_norm)
    return pl.pallas_call(
        kern,
        out_shape=jax.ShapeDtypeStruct(x2.shape, x2.dtype),
        grid=(n // tm,),
        in_specs=[
            pl.BlockSpec((tm, d), lambda i: (i, 0)),
            _resident((1, d), layer),
            _resident((d, D_FF), layer),
            _resident((D_FF, d), layer),
            pl.BlockSpec((1, d), lambda i: (0, 0)),
        ],
        out_specs=pl.BlockSpec((tm, d), lambda i: (i, 0)),
        scratch_shapes=[pltpu.VMEM((tm, d), BF16), pltpu.VMEM((tm, D_FF), BF16)],
        compiler_params=pltpu.CompilerParams(
            dimension_semantics=("arbitrary",), vmem_limit_bytes=vmem_limit),
        name=f"mlp_l{layer}",
    )(x2, p["norm_mlp_w"], p["w_up"], p["w_down"], p["final_norm_w"])


def _even_odd_lanes(v):
    pad = [(0, 0)] * (v.ndim - 1) + [(0, LANES - PAIRS)]
    return jnp.concatenate([jnp.pad(v[..., 0::2], pad), jnp.pad(v[..., 1::2], pad)], axis=-1)


def _pad_rows(w, rows):
    return jnp.pad(w, ((0, 0), (0, rows - w.shape[1]), (0, 0)))


def kernel(x, norm_mix_w, w_in, short_conv_w, ssd_conv_w, ssd_conv_b, dt_bias, a_log, d_skip, ssd_norm_w,
           w_out, norm_mlp_w, w_up, w_down, final_norm_w):
    depth = w_in.shape[0]
    b, t, d = x.shape
    tt = min(256, t)
    tm = min(512, b * t)
    assert d == D_MODEL and t % tt == 0 and tt % CHUNK == 0 and (b * t) % tm == 0
    vmem_limit = 56 * 1024 * 1024
    p = {
        "norm_mix_w": norm_mix_w[:, None, :],
        "w_main": w_in[:, :, :MAIN_COLS].astype(BF16),
        "w_dt": _even_odd_lanes(w_in[:, :, MAIN_COLS:]).astype(BF16),
        "short_conv_w": _pad_rows(short_conv_w, HIST),
        "ssd_conv_w": _pad_rows(ssd_conv_w, HIST),
        "ssd_conv_b": ssd_conv_b[:, None, :],
        "dt_bias": _even_odd_lanes(dt_bias)[:, None, :],
        "a_log": _even_odd_lanes(a_log)[:, None, :],
        "d_skip": jnp.repeat(d_skip, SSD_HEAD_DIM, axis=-1)[:, None, :],
        "ssd_norm_w": ssd_norm_w[:, None, :],
        "w_out": w_out.astype(BF16),
        "norm_mlp_w": norm_mlp_w[:, None, :],
        "w_up": w_up.astype(BF16),
        "w_down": w_down.astype(BF16),
        "final_norm_w": final_norm_w[None, :],
    }
    x2 = x.reshape(b * t, d)
    for layer in range(depth):
        x2 = _mixer_call(x2, p, layer, tt, t // tt, vmem_limit)
        x2 = _mlp_call(x2, p, layer, tm, vmem_limit, layer == depth - 1)
    return x2.reshape(b, t, d)
```

```python
import functools

import jax
import jax.numpy as jnp
from jax import lax
from jax.experimental import pallas as pl
from jax.experimental.pallas import tpu as pltpu

D_MODEL = 1024
CHUNK = 64
EPS = 1e-5
CONV_WIDTH = 1024
SHORT_K = 3
SSD_HEAD_DIM = 64
SSD_HEADS = 16
SSD_INNER = 1024
SSD_GROUPS = 2
SSD_STATE = 128
SSD_CONV_K = 4
SSD_CONV_DIM = SSD_INNER + 2 * SSD_GROUPS * SSD_STATE
MIX_WIDTH = CONV_WIDTH + SSD_INNER
D_FF = 4 * D_MODEL
MAIN_COLS = 3 * CONV_WIDTH + SSD_INNER + SSD_CONV_DIM
COL_UB, COL_UC, COL_UH, COL_Z, COL_XBC = 0, 1024, 2048, 3072, 4096
XBC_B = SSD_INNER
XBC_C = SSD_INNER + SSD_GROUPS * SSD_STATE
LANES = 128
HIST = 8
PAIRS = SSD_HEADS // 2
PAIRS_PER_GROUP = PAIRS // SSD_GROUPS
GROUP_COLS = SSD_INNER // SSD_GROUPS

F32 = jnp.float32
BF16 = jnp.bfloat16


def _rms(x, w):
    ms = jnp.mean(x * x, axis=-1, keepdims=True)
    return x * lax.rsqrt(ms + EPS) * w


def _softplus(x):
    return jnp.maximum(x, 0.0) + jnp.log1p(jnp.exp(-jnp.abs(x)))


def _silu(x):
    return x * jax.nn.sigmoid(x)


def _mixer_kernel(x_ref, xn_ref, nw_ref, wmain_ref, wdt_ref, scw_ref, ssw_ref, ssb_ref, dtb_ref, alog_ref,
                  dskip_ref, gnw_ref, wout_ref, o_ref,
                  h_ref, pj_ref, vext_ref, xbcext_ref, dtraw_ref, xbc_ref, zs_ref, dt_ref, a_ref, ybuf_ref,
                  ymix_ref, s_ref, *, tt, nc, rb, tiles_per_seq):
    g = pl.program_id(0)

    def norm_rows(src_ref, r0):
        h_ref[r0:r0 + rb, :] = _rms(src_ref[r0:r0 + rb, :], nw_ref[...]).astype(BF16)

    def project_cols(c0):
        if c0 < COL_XBC:
            pj_ref[:, c0:c0 + nc] = jnp.dot(h_ref[...], wmain_ref[:, c0:c0 + nc], preferred_element_type=F32)
        elif c0 < MAIN_COLS:
            xbcext_ref[HIST:HIST + tt, c0 - COL_XBC:c0 - COL_XBC + nc] = jnp.dot(
                h_ref[...], wmain_ref[:, c0:c0 + nc], preferred_element_type=F32)
        else:
            dtraw_ref[...] = jnp.dot(h_ref[...], wdt_ref[...], preferred_element_type=F32)

    proj_blocks = list(range(0, MAIN_COLS, nc)) + [MAIN_COLS]
    row_blocks = list(range(0, tt, rb))

    def gate_rows(r0):
        zs_ref[r0:r0 + rb, :] = _silu(pj_ref[r0:r0 + rb, COL_Z:COL_Z + SSD_INNER])

    def ssd_conv_rows(r0):
        acc = ssb_ref[...] + ssw_ref[SSD_CONV_K - 1:SSD_CONV_K, :] * xbcext_ref[HIST + r0:HIST + r0 + rb, :]
        for k in range(1, SSD_CONV_K):
            acc = acc + (ssw_ref[SSD_CONV_K - 1 - k:SSD_CONV_K - k, :]
                         * xbcext_ref[HIST + r0 - k:HIST + r0 - k + rb, :])
        xbc_ref[r0:r0 + rb, :] = _silu(acc)

    def carry_xbc_history(tile):
        next_starts_seq = (tile + 1) % tiles_per_seq == 0
        xbcext_ref[0:HIST, :] = jnp.where(next_starts_seq, 0.0, xbcext_ref[tt:tt + HIST, :])

    def dt_stage():
        dt = _softplus(dtraw_ref[...] + dtb_ref[...])
        dt_ref[...] = dt
        a_ref[...] = dt * (-jnp.exp(alog_ref[...]))

    @pl.when(g == 0)
    def _():
        for r0 in row_blocks:
            norm_rows(x_ref, r0)
        for c0 in proj_blocks:
            project_cols(c0)
        xbcext_ref[0:HIST, :] = jnp.zeros((HIST, SSD_CONV_DIM), F32)
        for r0 in row_blocks:
            gate_rows(r0)
            ssd_conv_rows(r0)
        carry_xbc_history(g)
        dt_stage()

    @pl.when(g % tiles_per_seq == 0)
    def _():
        vext_ref[0:HIST, :] = jnp.zeros((HIST, CONV_WIDTH), F32)
        s_ref[...] = jnp.zeros(s_ref.shape, F32)

    def short_conv_rows(r0):
        acc = scw_ref[SHORT_K - 1:SHORT_K, :] * vext_ref[HIST + r0:HIST + r0 + rb, :]
        for k in range(1, SHORT_K):
            acc = acc + scw_ref[SHORT_K - 1 - k:SHORT_K - k, :] * vext_ref[HIST + r0 - k:HIST + r0 - k + rb, :]
        ymix_ref[r0:r0 + rb, 0:CONV_WIDTH] = (pj_ref[r0:r0 + rb, COL_UB:COL_UB + CONV_WIDTH] * acc).astype(BF16)

    row_i = lax.broadcasted_iota(jnp.int32, (2 * CHUNK, LANES), 0)
    col_i = lax.broadcasted_iota(jnp.int32, (2 * CHUNK, LANES), 1)
    same_half = (row_i >= CHUNK) == (col_i >= CHUNK)
    tril2 = jnp.where(same_half & (col_i <= row_i), 1.0, 0.0).astype(F32)
    sub_i = lax.broadcasted_iota(jnp.int32, (CHUNK, LANES), 0)
    lane_i = lax.broadcasted_iota(jnp.int32, (CHUNK, LANES), 1)
    causal2 = (lane_i & (CHUNK - 1)) <= sub_i
    left_half = lane_i < CHUNK

    def pair_cols(zmat, j):
        even = jnp.broadcast_to(zmat[0:CHUNK, j:j + 1], (CHUNK, LANES))
        odd = jnp.broadcast_to(zmat[CHUNK:2 * CHUNK, j:j + 1], (CHUNK, LANES))
        return jnp.where(left_half, even, odd)

    def chunk_prep(c):
        rows = slice(c * CHUNK, (c + 1) * CHUNK)
        a_z = jnp.concatenate([a_ref[rows, 0:LANES], a_ref[rows, LANES:2 * LANES]], axis=0)
        dt_z = jnp.concatenate([dt_ref[rows, 0:LANES], dt_ref[rows, LANES:2 * LANES]], axis=0)
        acum_z = jnp.dot(tril2, a_z, precision=lax.Precision.HIGHEST, preferred_element_type=F32)
        return rows, acum_z, dt_z, acum_z.T, dt_z.T

    def chunk_group(prep, grp):
        rows, acum_z, dt_z, acum_r, dt_r = prep
        c_g = xbc_ref[rows, XBC_C + grp * SSD_STATE:XBC_C + (grp + 1) * SSD_STATE]
        b_g = xbc_ref[rows, XBC_B + grp * SSD_STATE:XBC_B + (grp + 1) * SSD_STATE]
        c_bf = c_g.astype(BF16)
        b2 = jnp.concatenate([b_g, b_g], axis=0)
        g2 = lax.dot_general(c_bf, b2.astype(BF16), (((1,), (1,)), ((), ())),
                             preferred_element_type=F32)
        b_t = b2.T[:, 0:CHUNK].astype(BF16)
        s_g = s_ref[grp]
        cs = jnp.dot(c_bf, s_g.astype(BF16), preferred_element_type=F32)
        pairs = [grp * PAIRS_PER_GROUP + jj for jj in range(PAIRS_PER_GROUP)]
        col_a = [pair_cols(acum_z, j) for j in pairs]
        col_dt = [pair_cols(dt_z, j) for j in pairs]
        xs2 = [xbc_ref[rows, j * LANES:(j + 1) * LANES] for j in pairs]
        y_diag = []
        for jj, j in enumerate(pairs):
            decay = jnp.exp(jnp.where(causal2, col_a[jj] - acum_r[j:j + 1, :], -jnp.inf))
            m2 = (g2 * decay * dt_r[j:j + 1, :]).astype(BF16)
            xx = jnp.where(same_half, jnp.concatenate([xs2[jj], xs2[jj]], axis=0), 0.0).astype(BF16)
            y_diag.append(jnp.dot(m2, xx, preferred_element_type=F32))
        xd, chunk_decay = [], []
        for jj, j in enumerate(pairs):
            cols = slice(j * LANES, (j + 1) * LANES)
            y = y_diag[jj] + cs[:, jj * LANES:(jj + 1) * LANES] * jnp.exp(col_a[jj])
            y = y + dskip_ref[:, cols] * xs2[jj]
            ybuf_ref[rows, cols] = y * zs_ref[rows, cols]
            last_a = col_a[jj][CHUNK - 1:CHUNK, :]
            xd.append((xs2[jj] * col_dt[jj] * jnp.exp(last_a - col_a[jj])).astype(BF16))
            chunk_decay.append(jnp.exp(last_a))
        ds = jnp.dot(b_t, jnp.concatenate(xd, axis=1), preferred_element_type=F32)
        s_ref[grp] = s_g * jnp.concatenate(chunk_decay, axis=1) + ds

    def group_norm_rows(r0):
        for grp in range(SSD_GROUPS):
            gc = slice(grp * GROUP_COLS, (grp + 1) * GROUP_COLS)
            yg = _rms(ybuf_ref[r0:r0 + rb, gc], gnw_ref[:, gc])
            ymix_ref[r0:r0 + rb, CONV_WIDTH + grp * GROUP_COLS:CONV_WIDTH + (grp + 1) * GROUP_COLS] = (
                yg.astype(BF16))

    pending = [c0 for c0 in proj_blocks if c0 >= COL_Z] + [c0 for c0 in proj_blocks if c0 < COL_Z]

    def project_next(n):
        for _ in range(n):
            if pending:
                project_cols(pending.pop(0))

    half = len(row_blocks) // 2
    for r0 in row_blocks:
        norm_rows(xn_ref, r0)
    project_next(1)
    for r0 in row_blocks:
        vext_ref[HIST + r0:HIST + r0 + rb, :] = (
            pj_ref[r0:r0 + rb, COL_UC:COL_UC + CONV_WIDTH] * pj_ref[r0:r0 + rb, COL_UH:COL_UH + CONV_WIDTH])
    project_next(1)
    for r0 in row_blocks[:half]:
        short_conv_rows(r0)
    project_next(1)
    for r0 in row_blocks[half:]:
        short_conv_rows(r0)
    vext_ref[0:HIST, :] = vext_ref[tt:tt + HIST, :]
    n_chunks = tt // CHUNK
    for c in range(n_chunks):
        prep = chunk_prep(c)
        for grp in range(SSD_GROUPS):
            project_next(1)
            chunk_group(prep, grp)
    project_next(len(pending))
    for r0 in row_blocks:
        group_norm_rows(r0)
    ahead = [functools.partial(gate_rows, r0) for r0 in row_blocks]
    ahead += [functools.partial(ssd_conv_rows, r0) for r0 in row_blocks]
    out_blocks = list(range(0, D_MODEL, nc))
    per_block = -(-len(ahead) // len(out_blocks))
    for n0 in out_blocks:
        o_ref[:, n0:n0 + nc] = x_ref[:, n0:n0 + nc] + jnp.dot(
            ymix_ref[...], wout_ref[:, n0:n0 + nc], preferred_element_type=F32)
        for stage in ahead[:per_block]:
            stage()
        del ahead[:per_block]
    carry_xbc_history(g + 1)
    dt_stage()


def _mlp_kernel(x_ref, nw_ref, wup_ref, wdn_ref, fw_ref, o_ref, h_ref, hid_ref, *, tm, nc, rb, final_norm):
    for r0 in range(0, tm, rb):
        h_ref[r0:r0 + rb, :] = _rms(x_ref[r0:r0 + rb, :], nw_ref[...]).astype(BF16)
    for f0 in range(0, D_FF, nc):
        u = jnp.maximum(jnp.dot(h_ref[...], wup_ref[:, f0:f0 + nc], preferred_element_type=F32), 0.0)
        hid_ref[:, f0:f0 + nc] = (u * u).astype(BF16)
    for n0 in range(0, D_MODEL, nc):
        o_ref[:, n0:n0 + nc] = x_ref[:, n0:n0 + nc] + jnp.dot(
            hid_ref[...], wdn_ref[:, n0:n0 + nc], preferred_element_type=F32)
    if final_norm:
        for r0 in range(0, tm, rb):
            o_ref[r0:r0 + rb, :] = _rms(o_ref[r0:r0 + rb, :], fw_ref[...])


def _resident(shape, layer):
    nd = len(shape)
    return pl.BlockSpec((None,) + tuple(shape), lambda *_: (layer,) + (0,) * nd,
                        pipeline_mode=pl.Buffered(1))


def _mixer_call(x2, p, layer, tt, tiles_per_seq, vmem_limit):
    n, d = x2.shape
    n_tiles = n // tt
    kern = functools.partial(_mixer_kernel, tt=tt, nc=512, rb=32, tiles_per_seq=tiles_per_seq)
    in_specs = [
        pl.BlockSpec((tt, d), lambda g: (g, 0)),
        pl.BlockSpec((tt, d), lambda g: (jnp.minimum(g + 1, n_tiles - 1), 0)),
        _resident((1, d), layer),
        _resident((d, MAIN_COLS), layer),
        _resident((d, 2 * LANES), layer),
        _resident((HIST, CONV_WIDTH), layer),
        _resident((HIST, SSD_CONV_DIM), layer),
        _resident((1, SSD_CONV_DIM), layer),
        _resident((1, 2 * LANES), layer),
        _resident((1, 2 * LANES), layer),
        _resident((1, SSD_INNER), layer),
        _resident((1, SSD_INNER), layer),
        _resident((MIX_WIDTH, d), layer),
    ]
    scratch = [
        pltpu.VMEM((tt, d), BF16),
        pltpu.VMEM((tt, COL_XBC), F32),
        pltpu.VMEM((tt + HIST, CONV_WIDTH), F32),
        pltpu.VMEM((tt + HIST, SSD_CONV_DIM), F32),
        pltpu.VMEM((tt, 2 * LANES), F32),
        pltpu.VMEM((tt, SSD_CONV_DIM), F32),
        pltpu.VMEM((tt, SSD_INNER), F32),
        pltpu.VMEM((tt, 2 * LANES), F32),
        pltpu.VMEM((tt, 2 * LANES), F32),
        pltpu.VMEM((tt, SSD_INNER), F32),
        pltpu.VMEM((tt, MIX_WIDTH), BF16),
        pltpu.VMEM((SSD_GROUPS, SSD_STATE, GROUP_COLS), F32),
    ]
    return pl.pallas_call(
        kern,
        out_shape=jax.ShapeDtypeStruct(x2.shape, x2.dtype),
        grid=(n_tiles,),
        in_specs=in_specs,
        out_specs=pl.BlockSpec((tt, d), lambda g: (g, 0)),
        scratch_shapes=scratch,
        compiler_params=pltpu.CompilerParams(
            dimension_semantics=("arbitrary",), vmem_limit_bytes=vmem_limit),
        name=f"mixer_l{layer}",
    )(x2, x2, p["norm_mix_w"], p["w_main"], p["w_dt"], p["short_conv_w"], p["ssd_conv_w"], p["ssd_conv_b"],
      p["dt_bias"], p["a_log"], p["d_skip"], p["ssd_norm_w"], p["w_out"])


def _mlp_call(x2, p, layer, tm, vmem_limit, final_norm):
    n, d = x2.shape
    kern = functools.partial(_mlp_kernel, tm=tm, nc=512, rb=32, final_norm=final_norm)
    return pl.pallas_call(
        kern,
        out_shape=jax.ShapeDtypeStruct(x2.shape, x2.dtype),
        grid=(n // tm,),
        in_specs=[
            pl.BlockSpec((tm, d), lambda i: (i, 0)),
            _resident((1, d), layer),
            _resident((d, D_FF), layer),
            _resident((D_FF, d), layer),
            pl.BlockSpec((1, d), lambda i: (0, 0)),
        ],
        out_specs=pl.BlockSpec((tm, d), lambda i: (i, 0)),
        scratch_shapes=[pltpu.VMEM((tm, d), BF16), pltpu.VMEM((tm, D_FF), BF16)],
        compiler_params=pltpu.CompilerParams(
            dimension_semantics=("arbitrary",), vmem_limit_bytes=vmem_limit),
        name=f"mlp_l{layer}",
    )(x2, p["norm_mlp_w"], p["w_up"], p["w_down"], p["final_norm_w"])


def _even_odd_lanes(v):
    pad = [(0, 0)] * (v.ndim - 1) + [(0, LANES - PAIRS)]
    return jnp.concatenate([jnp.pad(v[..., 0::2], pad), jnp.pad(v[..., 1::2], pad)], axis=-1)


def _pad_rows(w, rows):
    return jnp.pad(w, ((0, 0), (0, rows - w.shape[1]), (0, 0)))


def kernel(x, norm_mix_w, w_in, short_conv_w, ssd_conv_w, ssd_conv_b, dt_bias, a_log, d_skip, ssd_norm_w,
           w_out, norm_mlp_w, w_up, w_down, final_norm_w):
    depth = w_in.shape[0]
    b, t, d = x.shape
    tt = min(256, t)
    tm = min(512, b * t)
    assert d == D_MODEL and t % tt == 0 and tt % CHUNK == 0 and (b * t) % tm == 0
    vmem_limit = 56 * 1024 * 1024
    p = {
        "norm_mix_w": norm_mix_w[:, None, :],
        "w_main": w_in.astype(BF16),
        "w_dt": _even_odd_lanes(w_in[:, :, MAIN_COLS:]).astype(BF16),
        "short_conv_w": _pad_rows(short_conv_w, HIST),
        "ssd_conv_w": _pad_rows(ssd_conv_w, HIST),
        "ssd_conv_b": ssd_conv_b[:, None, :],
        "dt_bias": _even_odd_lanes(dt_bias)[:, None, :],
        "a_log": _even_odd_lanes(a_log)[:, None, :],
        "d_skip": jnp.repeat(d_skip, SSD_HEAD_DIM, axis=-1)[:, None, :],
        "ssd_norm_w": ssd_norm_w[:, None, :],
        "w_out": w_out.astype(BF16),
        "norm_mlp_w": norm_mlp_w[:, None, :],
        "w_up": w_up.astype(BF16),
        "w_down": w_down.astype(BF16),
        "final_norm_w": final_norm_w[None, :],
    }
    x2 = x.reshape(b * t, d)
    for layer in range(depth):
        x2 = _mixer_call(x2, p, layer, tt, t // tt, vmem_limit)
        x2 = _mlp_call(x2, p, layer, tm, vmem_limit, layer == depth - 1)
    return x2.reshape(b, t, d)
```

```python
import functools

import jax
import jax.numpy as jnp
from jax import lax
from jax.experimental import pallas as pl
from jax.experimental.pallas import tpu as pltpu

D_MODEL = 1024
CHUNK = 64
EPS = 1e-5
CONV_WIDTH = 1024
SHORT_K = 3
SSD_HEAD_DIM = 64
SSD_HEADS = 16
SSD_INNER = 1024
SSD_GROUPS = 2
SSD_STATE = 128
SSD_CONV_K = 4
SSD_CONV_DIM = SSD_INNER + 2 * SSD_GROUPS * SSD_STATE
MIX_WIDTH = CONV_WIDTH + SSD_INNER
D_FF = 4 * D_MODEL
MAIN_COLS = 3 * CONV_WIDTH + SSD_INNER + SSD_CONV_DIM
COL_UB, COL_UC, COL_UH, COL_Z, COL_XBC = 0, 1024, 2048, 3072, 4096
XBC_B = SSD_INNER
XBC_C = SSD_INNER + SSD_GROUPS * SSD_STATE
LANES = 128
HIST = 8
BF16_ROWS = 16
PAIRS = SSD_HEADS // 2
PAIRS_PER_GROUP = PAIRS // SSD_GROUPS
GROUP_COLS = SSD_INNER // SSD_GROUPS

F32 = jnp.float32
BF16 = jnp.bfloat16


def _rms(x, w):
    ms = jnp.mean(x * x, axis=-1, keepdims=True)
    return x * lax.rsqrt(ms + EPS) * w


def _softplus(x):
    return jnp.maximum(x, 0.0) + jnp.log1p(jnp.exp(-jnp.abs(x)))


def _silu(x):
    return x * jax.nn.sigmoid(x)


def _mixer_kernel(x_ref, xn_ref, nw_ref, wmain_ref, wdt_ref, scw_ref, ssw_ref, ssb_ref, dtb_ref, alog_ref,
                  dskip_ref, gnw_ref, wout_ref, wup32_ref, wdn32_ref, o_ref, wup16_ref, wdn16_ref,
                  h_ref, pj_ref, vext_ref, xbcext_ref, dtraw_ref, xbc_ref, zs_ref, dt_ref, a_ref, ybuf_ref,
                  ymix_ref, s_ref, *, tt, nc, rb, tiles_per_seq):
    g = pl.program_id(0)

    wup16_ref[...] = wup32_ref[...].astype(BF16)
    wdn16_ref[...] = wdn32_ref[...].astype(BF16)

    def norm_rows(src_ref, r0):
        h_ref[r0:r0 + rb, :] = _rms(src_ref[r0:r0 + rb, :], nw_ref[...]).astype(BF16)

    def project_cols(c0):
        if c0 < COL_XBC:
            pj_ref[:, c0:c0 + nc] = jnp.dot(h_ref[...], wmain_ref[:, c0:c0 + nc], preferred_element_type=F32)
        elif c0 < MAIN_COLS:
            xbcext_ref[HIST:HIST + tt, c0 - COL_XBC:c0 - COL_XBC + nc] = jnp.dot(
                h_ref[...], wmain_ref[:, c0:c0 + nc], preferred_element_type=F32)
        else:
            dtraw_ref[...] = jnp.dot(h_ref[...], wdt_ref[...], preferred_element_type=F32)

    proj_blocks = list(range(0, MAIN_COLS, nc)) + [MAIN_COLS]
    row_blocks = list(range(0, tt, rb))

    def gate_rows(r0):
        zs_ref[r0:r0 + rb, :] = _silu(pj_ref[r0:r0 + rb, COL_Z:COL_Z + SSD_INNER])

    def ssd_conv_rows(r0):
        acc = ssb_ref[...] + ssw_ref[SSD_CONV_K - 1:SSD_CONV_K, :] * xbcext_ref[HIST + r0:HIST + r0 + rb, :]
        for k in range(1, SSD_CONV_K):
            acc = acc + (ssw_ref[SSD_CONV_K - 1 - k:SSD_CONV_K - k, :]
                         * xbcext_ref[HIST + r0 - k:HIST + r0 - k + rb, :])
        xbc_ref[r0:r0 + rb, :] = _silu(acc)

    def carry_xbc_history(tile):
        next_starts_seq = (tile + 1) % tiles_per_seq == 0
        xbcext_ref[0:HIST, :] = jnp.where(next_starts_seq, 0.0, xbcext_ref[tt:tt + HIST, :])

    def dt_stage():
        dt = _softplus(dtraw_ref[...] + dtb_ref[...])
        dt_ref[...] = dt
        a_ref[...] = dt * (-jnp.exp(alog_ref[...]))

    @pl.when(g == 0)
    def _():
        for r0 in row_blocks:
            norm_rows(x_ref, r0)
        for c0 in proj_blocks:
            project_cols(c0)
        xbcext_ref[0:HIST, :] = jnp.zeros((HIST, SSD_CONV_DIM), F32)
        for r0 in row_blocks:
            gate_rows(r0)
            ssd_conv_rows(r0)
        carry_xbc_history(g)
        dt_stage()

    @pl.when(g % tiles_per_seq == 0)
    def _():
        vext_ref[0:HIST, :] = jnp.zeros((HIST, CONV_WIDTH), F32)
        s_ref[...] = jnp.zeros(s_ref.shape, F32)

    def short_conv_rows(r0):
        acc = scw_ref[SHORT_K - 1:SHORT_K, :] * vext_ref[HIST + r0:HIST + r0 + rb, :]
        for k in range(1, SHORT_K):
            acc = acc + scw_ref[SHORT_K - 1 - k:SHORT_K - k, :] * vext_ref[HIST + r0 - k:HIST + r0 - k + rb, :]
        ymix_ref[r0:r0 + rb, 0:CONV_WIDTH] = (pj_ref[r0:r0 + rb, COL_UB:COL_UB + CONV_WIDTH] * acc).astype(BF16)

    row_i = lax.broadcasted_iota(jnp.int32, (2 * CHUNK, LANES), 0)
    col_i = lax.broadcasted_iota(jnp.int32, (2 * CHUNK, LANES), 1)
    same_half = (row_i >= CHUNK) == (col_i >= CHUNK)
    tril2 = jnp.where(same_half & (col_i <= row_i), 1.0, 0.0).astype(F32)
    sub_i = lax.broadcasted_iota(jnp.int32, (CHUNK, LANES), 0)
    lane_i = lax.broadcasted_iota(jnp.int32, (CHUNK, LANES), 1)
    causal2 = (lane_i & (CHUNK - 1)) <= sub_i
    left_half = lane_i < CHUNK

    def pair_cols(zmat, j):
        even = jnp.broadcast_to(zmat[0:CHUNK, j:j + 1], (CHUNK, LANES))
        odd = jnp.broadcast_to(zmat[CHUNK:2 * CHUNK, j:j + 1], (CHUNK, LANES))
        return jnp.where(left_half, even, odd)

    def chunk_prep(c):
        rows = slice(c * CHUNK, (c + 1) * CHUNK)
        a_z = jnp.concatenate([a_ref[rows, 0:LANES], a_ref[rows, LANES:2 * LANES]], axis=0)
        dt_z = jnp.concatenate([dt_ref[rows, 0:LANES], dt_ref[rows, LANES:2 * LANES]], axis=0)
        acum_z = jnp.dot(tril2, a_z, precision=lax.Precision.HIGHEST, preferred_element_type=F32)
        return rows, acum_z, dt_z, acum_z.T, dt_z.T

    def chunk_group(prep, grp):
        rows, acum_z, dt_z, acum_r, dt_r = prep
        c_g = xbc_ref[rows, XBC_C + grp * SSD_STATE:XBC_C + (grp + 1) * SSD_STATE]
        b_g = xbc_ref[rows, XBC_B + grp * SSD_STATE:XBC_B + (grp + 1) * SSD_STATE]
        c_bf = c_g.astype(BF16)
        b2 = jnp.concatenate([b_g, b_g], axis=0)
        g2 = lax.dot_general(c_bf, b2.astype(BF16), (((1,), (1,)), ((), ())),
                             preferred_element_type=F32)
        b_t = b2.T[:, 0:CHUNK].astype(BF16)
        s_g = s_ref[grp]
        cs = jnp.dot(c_bf, s_g.astype(BF16), preferred_element_type=F32)
        pairs = [grp * PAIRS_PER_GROUP + jj for jj in range(PAIRS_PER_GROUP)]
        col_a = [pair_cols(acum_z, j) for j in pairs]
        col_dt = [pair_cols(dt_z, j) for j in pairs]
        xs2 = [xbc_ref[rows, j * LANES:(j + 1) * LANES] for j in pairs]
        y_diag = []
        for jj, j in enumerate(pairs):
            decay = jnp.exp(jnp.where(causal2, col_a[jj] - acum_r[j:j + 1, :], -jnp.inf))
            m2 = (g2 * decay * dt_r[j:j + 1, :]).astype(BF16)
            xx = jnp.where(same_half, jnp.concatenate([xs2[jj], xs2[jj]], axis=0), 0.0).astype(BF16)
            y_diag.append(jnp.dot(m2, xx, preferred_element_type=F32))
        xd, chunk_decay = [], []
        for jj, j in enumerate(pairs):
            cols = slice(j * LANES, (j + 1) * LANES)
            y = y_diag[jj] + cs[:, jj * LANES:(jj + 1) * LANES] * jnp.exp(col_a[jj])
            y = y + dskip_ref[:, cols] * xs2[jj]
            ybuf_ref[rows, cols] = y * zs_ref[rows, cols]
            last_a = col_a[jj][CHUNK - 1:CHUNK, :]
            xd.append((xs2[jj] * col_dt[jj] * jnp.exp(last_a - col_a[jj])).astype(BF16))
            chunk_decay.append(jnp.exp(last_a))
        ds = jnp.dot(b_t, jnp.concatenate(xd, axis=1), preferred_element_type=F32)
        s_ref[grp] = s_g * jnp.concatenate(chunk_decay, axis=1) + ds

    def group_norm_rows(r0):
        for grp in range(SSD_GROUPS):
            gc = slice(grp * GROUP_COLS, (grp + 1) * GROUP_COLS)
            yg = _rms(ybuf_ref[r0:r0 + rb, gc], gnw_ref[:, gc])
            ymix_ref[r0:r0 + rb, CONV_WIDTH + grp * GROUP_COLS:CONV_WIDTH + (grp + 1) * GROUP_COLS] = (
                yg.astype(BF16))

    pending = [c0 for c0 in proj_blocks if c0 >= COL_Z] + [c0 for c0 in proj_blocks if c0 < COL_Z]

    def project_next(n):
        for _ in range(n):
            if pending:
                project_cols(pending.pop(0))

    half = len(row_blocks) // 2
    for r0 in row_blocks:
        norm_rows(xn_ref, r0)
    project_next(1)
    for r0 in row_blocks:
        vext_ref[HIST + r0:HIST + r0 + rb, :] = (
            pj_ref[r0:r0 + rb, COL_UC:COL_UC + CONV_WIDTH] * pj_ref[r0:r0 + rb, COL_UH:COL_UH + CONV_WIDTH])
    project_next(1)
    for r0 in row_blocks[:half]:
        short_conv_rows(r0)
    project_next(1)
    for r0 in row_blocks[half:]:
        short_conv_rows(r0)
    vext_ref[0:HIST, :] = vext_ref[tt:tt + HIST, :]
    n_chunks = tt // CHUNK
    for c in range(n_chunks):
        prep = chunk_prep(c)
        for grp in range(SSD_GROUPS):
            project_next(1)
            chunk_group(prep, grp)
    project_next(len(pending))
    for r0 in row_blocks:
        group_norm_rows(r0)
    ahead = [functools.partial(gate_rows, r0) for r0 in row_blocks]
    ahead += [functools.partial(ssd_conv_rows, r0) for r0 in row_blocks]
    out_blocks = list(range(0, D_MODEL, nc))
    per_block = -(-len(ahead) // len(out_blocks))
    for n0 in out_blocks:
        o_ref[:, n0:n0 + nc] = x_ref[:, n0:n0 + nc] + jnp.dot(
            ymix_ref[...], wout_ref[:, n0:n0 + nc], preferred_element_type=F32)
        for stage in ahead[:per_block]:
            stage()
        del ahead[:per_block]
    carry_xbc_history(g + 1)
    dt_stage()


def _mlp_kernel(x_ref, nw_ref, wup_ref, wdn_ref, fw_ref, win32_ref, wout32_ref, o_ref, win16_ref, wout16_ref,
                h_ref, hid_ref, *, tm, nc, rb, final_norm):
    win16_ref[...] = win32_ref[...].astype(BF16)
    wout16_ref[...] = wout32_ref[...].astype(BF16)
    for r0 in range(0, tm, rb):
        h_ref[r0:r0 + rb, :] = _rms(x_ref[r0:r0 + rb, :], nw_ref[...]).astype(BF16)
    for f0 in range(0, D_FF, nc):
        u = jnp.maximum(jnp.dot(h_ref[...], wup_ref[:, f0:f0 + nc], preferred_element_type=F32), 0.0)
        hid_ref[:, f0:f0 + nc] = (u * u).astype(BF16)
    for n0 in range(0, D_MODEL, nc):
        o_ref[:, n0:n0 + nc] = x_ref[:, n0:n0 + nc] + jnp.dot(
            hid_ref[...], wdn_ref[:, n0:n0 + nc], preferred_element_type=F32)
    if final_norm:
        for r0 in range(0, tm, rb):
            o_ref[r0:r0 + rb, :] = _rms(o_ref[r0:r0 + rb, :], fw_ref[...])


def _resident(shape, layer):
    nd = len(shape)
    return pl.BlockSpec((None,) + tuple(shape), lambda *_: (layer,) + (0,) * nd,
                        pipeline_mode=pl.Buffered(1))


def _whole(shape):
    return pl.BlockSpec(tuple(shape), lambda *_: (0,) * len(shape), pipeline_mode=pl.Buffered(1))


def _cast_specs(shape, steps, layer):
    rows, cols = shape
    n_blocks = 1
    while n_blocks * 2 <= min(steps, rows // BF16_ROWS):
        n_blocks *= 2
    assert rows % n_blocks == 0 and (rows // n_blocks) % BF16_ROWS == 0
    br = rows // n_blocks
    src = pl.BlockSpec((None, br, cols), lambda i: (layer, jnp.minimum(i, n_blocks - 1), 0))
    dst = pl.BlockSpec((br, cols), lambda i: (jnp.minimum(i, n_blocks - 1), 0))
    return src, dst


def _mixer_call(x2, p, w_main16, w_out16, w_up, w_down, layer, tt, tiles_per_seq, vmem_limit):
    n, d = x2.shape
    n_tiles = n // tt
    kern = functools.partial(_mixer_kernel, tt=tt, nc=512, rb=32, tiles_per_seq=tiles_per_seq)
    in_specs = [
        pl.BlockSpec((tt, d), lambda g: (g, 0)),
        pl.BlockSpec((tt, d), lambda g: (jnp.minimum(g + 1, n_tiles - 1), 0)),
        _resident((1, d), layer),
        pl.BlockSpec((d, MAIN_COLS), lambda g: (0, 0), pipeline_mode=pl.Buffered(1)),
        _resident((d, 2 * LANES), layer),
        _resident((HIST, CONV_WIDTH), layer),
        _resident((HIST, SSD_CONV_DIM), layer),
        _resident((1, SSD_CONV_DIM), layer),
        _resident((1, 2 * LANES), layer),
        _resident((1, 2 * LANES), layer),
        _resident((1, SSD_INNER), layer),
        _resident((1, SSD_INNER), layer),
        _whole((MIX_WIDTH, d)),
    ]
    scratch = [
        pltpu.VMEM((tt, d), BF16),
        pltpu.VMEM((tt, COL_XBC), F32),
        pltpu.VMEM((tt + HIST, CONV_WIDTH), F32),
        pltpu.VMEM((tt + HIST, SSD_CONV_DIM), F32),
        pltpu.VMEM((tt, 2 * LANES), F32),
        pltpu.VMEM((tt, SSD_CONV_DIM), F32),
        pltpu.VMEM((tt, SSD_INNER), F32),
        pltpu.VMEM((tt, 2 * LANES), F32),
        pltpu.VMEM((tt, 2 * LANES), F32),
        pltpu.VMEM((tt, SSD_INNER), F32),
        pltpu.VMEM((tt, MIX_WIDTH), BF16),
        pltpu.VMEM((SSD_GROUPS, SSD_STATE, GROUP_COLS), F32),
    ]
    up_in, up_out = _cast_specs(w_up.shape[1:], n_tiles, layer)
    dn_in, dn_out = _cast_specs(w_down.shape[1:], n_tiles, layer)
    return pl.pallas_call(
        kern,
        out_shape=(jax.ShapeDtypeStruct(x2.shape, x2.dtype),
                   jax.ShapeDtypeStruct(w_up.shape[1:], BF16), jax.ShapeDtypeStruct(w_down.shape[1:], BF16)),
        grid=(n_tiles,),
        in_specs=in_specs + [up_in, dn_in],
        out_specs=(pl.BlockSpec((tt, d), lambda g: (g, 0)), up_out, dn_out),
        scratch_shapes=scratch,
        compiler_params=pltpu.CompilerParams(
            dimension_semantics=("arbitrary",), vmem_limit_bytes=vmem_limit),
        name=f"mixer_l{layer}",
    )(x2, x2, p["norm_mix_w"], w_main16, p["w_dt"], p["short_conv_w"], p["ssd_conv_w"], p["ssd_conv_b"],
      p["dt_bias"], p["a_log"], p["d_skip"], p["ssd_norm_w"], w_out16, w_up, w_down)


def _mlp_call(x2, p, w_up16, w_down16, w_in, w_out, layer, next_layer, tm, vmem_limit, final_norm):
    n, d = x2.shape
    steps = n // tm
    kern = functools.partial(_mlp_kernel, tm=tm, nc=512, rb=32, final_norm=final_norm)
    in_in, in_out = _cast_specs(w_in.shape[1:], steps, next_layer)
    out_in, out_out = _cast_specs(w_out.shape[1:], steps, next_layer)
    return pl.pallas_call(
        kern,
        out_shape=(jax.ShapeDtypeStruct(x2.shape, x2.dtype),
                   jax.ShapeDtypeStruct(w_in.shape[1:], BF16), jax.ShapeDtypeStruct(w_out.shape[1:], BF16)),
        grid=(steps,),
        in_specs=[
            pl.BlockSpec((tm, d), lambda i: (i, 0)),
            _resident((1, d), layer),
            _whole((d, D_FF)),
            _whole((D_FF, d)),
            pl.BlockSpec((1, d), lambda i: (0, 0)),
            in_in, out_in,
        ],
        out_specs=(pl.BlockSpec((tm, d), lambda i: (i, 0)), in_out, out_out),
        scratch_shapes=[pltpu.VMEM((tm, d), BF16), pltpu.VMEM((tm, D_FF), BF16)],
        compiler_params=pltpu.CompilerParams(
            dimension_semantics=("arbitrary",), vmem_limit_bytes=vmem_limit),
        name=f"mlp_l{layer}",
    )(x2, p["norm_mlp_w"], w_up16, w_down16, p["final_norm_w"], w_in, w_out)


def _even_odd_lanes(v):
    pad = [(0, 0)] * (v.ndim - 1) + [(0, LANES - PAIRS)]
    return jnp.concatenate([jnp.pad(v[..., 0::2], pad), jnp.pad(v[..., 1::2], pad)], axis=-1)


def _pad_rows(w, rows):
    return jnp.pad(w, ((0, 0), (0, rows - w.shape[1]), (0, 0)))


def kernel(x, norm_mix_w, w_in, short_conv_w, ssd_conv_w, ssd_conv_b, dt_bias, a_log, d_skip, ssd_norm_w,
           w_out, norm_mlp_w, w_up, w_down, final_norm_w):
    depth = w_in.shape[0]
    b, t, d = x.shape
    tt = min(256, t)
    tm = min(512, b * t)
    assert d == D_MODEL and t % tt == 0 and tt % CHUNK == 0 and (b * t) % tm == 0
    vmem_limit = 56 * 1024 * 1024
    p = {
        "norm_mix_w": norm_mix_w[:, None, :],
        "w_dt": _even_odd_lanes(w_in[:, :, MAIN_COLS:]).astype(BF16),
        "short_conv_w": _pad_rows(short_conv_w, HIST),
        "ssd_conv_w": _pad_rows(ssd_conv_w, HIST),
        "ssd_conv_b": ssd_conv_b[:, None, :],
        "dt_bias": _even_odd_lanes(dt_bias)[:, None, :],
        "a_log": _even_odd_lanes(a_log)[:, None, :],
        "d_skip": jnp.repeat(d_skip, SSD_HEAD_DIM, axis=-1)[:, None, :],
        "ssd_norm_w": ssd_norm_w[:, None, :],
        "norm_mlp_w": norm_mlp_w[:, None, :],
        "final_norm_w": final_norm_w[None, :],
    }
    x2 = x.reshape(b * t, d)
    w_main16, w_out16 = w_in[0].astype(BF16), w_out[0].astype(BF16)
    for layer in range(depth):
        x2, w_up16, w_down16 = _mixer_call(x2, p, w_main16, w_out16, w_up, w_down, layer, tt, t // tt, vmem_limit)
        x2, w_main16, w_out16 = _mlp_call(x2, p, w_up16, w_down16, w_in, w_out, layer, (layer + 1) % depth, tm,
                                          vmem_limit, layer == depth - 1)
    return x2.reshape(b, t, d)
```

```python
import functools

import jax
import jax.numpy as jnp
from jax import lax
from jax.experimental import pallas as pl
from jax.experimental.pallas import tpu as pltpu

D_MODEL = 1024
CHUNK = 64
EPS = 1e-5
CONV_WIDTH = 1024
SHORT_K = 3
SSD_HEAD_DIM = 64
SSD_HEADS = 16
SSD_INNER = 1024
SSD_GROUPS = 2
SSD_STATE = 128
SSD_CONV_K = 4
SSD_CONV_DIM = SSD_INNER + 2 * SSD_GROUPS * SSD_STATE
MIX_WIDTH = CONV_WIDTH + SSD_INNER
D_FF = 4 * D_MODEL
MAIN_COLS = 3 * CONV_WIDTH + SSD_INNER + SSD_CONV_DIM
COL_UB, COL_UC, COL_UH, COL_Z, COL_XBC = 0, 1024, 2048, 3072, 4096
XBC_B = SSD_INNER
XBC_C = SSD_INNER + SSD_GROUPS * SSD_STATE
LANES = 128
HIST = 8
BF16_ROWS = 16
PAIRS = SSD_HEADS // 2
PAIRS_PER_GROUP = PAIRS // SSD_GROUPS
GROUP_COLS = SSD_INNER // SSD_GROUPS

F32 = jnp.float32
BF16 = jnp.bfloat16


def _rms(x, w):
    ms = jnp.mean(x * x, axis=-1, keepdims=True)
    return x * lax.rsqrt(ms + EPS) * w


def _softplus(x):
    return jnp.maximum(x, 0.0) + jnp.log1p(jnp.exp(-jnp.abs(x)))


def _silu(x):
    return x * jax.nn.sigmoid(x)


def _mixer_kernel(x_ref, xn_ref, nw_ref, wmain_ref, wdt_ref, scw_ref, ssw_ref, ssb_ref, dtb_ref, alog_ref,
                  dskip_ref, gnw_ref, wout_ref, wup32_ref, wdn32_ref, o_ref, wup16_ref, wdn16_ref,
                  h_ref, pj_ref, vext_ref, xbcext_ref, dtraw_ref, xbc_ref, zs_ref, dt_ref, a_ref, ybuf_ref,
                  ymix_ref, s_ref, *, tt, nc, rb, tiles_per_seq):
    g = pl.program_id(0)

    wup16_ref[...] = wup32_ref[...].astype(BF16)
    wdn16_ref[...] = wdn32_ref[...].astype(BF16)

    def norm_rows(src_ref, r0):
        h_ref[r0:r0 + rb, :] = _rms(src_ref[r0:r0 + rb, :], nw_ref[...]).astype(BF16)

    def project_cols(c0):
        if c0 < COL_XBC:
            pj_ref[:, c0:c0 + nc] = jnp.dot(h_ref[...], wmain_ref[:, c0:c0 + nc], preferred_element_type=F32)
        elif c0 < MAIN_COLS:
            xbcext_ref[HIST:HIST + tt, c0 - COL_XBC:c0 - COL_XBC + nc] = jnp.dot(
                h_ref[...], wmain_ref[:, c0:c0 + nc], preferred_element_type=F32)
        else:
            dtraw_ref[...] = jnp.dot(h_ref[...], wdt_ref[...], preferred_element_type=F32)

    proj_blocks = list(range(0, MAIN_COLS, nc)) + [MAIN_COLS]
    row_blocks = list(range(0, tt, rb))

    def gate_rows(r0):
        zs_ref[r0:r0 + rb, :] = _silu(pj_ref[r0:r0 + rb, COL_Z:COL_Z + SSD_INNER])

    def ssd_conv_rows(r0):
        acc = ssb_ref[...] + ssw_ref[SSD_CONV_K - 1:SSD_CONV_K, :] * xbcext_ref[HIST + r0:HIST + r0 + rb, :]
        for k in range(1, SSD_CONV_K):
            acc = acc + (ssw_ref[SSD_CONV_K - 1 - k:SSD_CONV_K - k, :]
                         * xbcext_ref[HIST + r0 - k:HIST + r0 - k + rb, :])
        xbc_ref[r0:r0 + rb, :] = _silu(acc)

    def carry_xbc_history(tile):
        next_starts_seq = (tile + 1) % tiles_per_seq == 0
        xbcext_ref[0:HIST, :] = jnp.where(next_starts_seq, 0.0, xbcext_ref[tt:tt + HIST, :])

    def dt_stage():
        dt = _softplus(dtraw_ref[...] + dtb_ref[...])
        dt_ref[...] = dt
        a_ref[...] = dt * (-jnp.exp(alog_ref[...]))

    @pl.when(g == 0)
    def _():
        for r0 in row_blocks:
            norm_rows(x_ref, r0)
        for c0 in proj_blocks:
            project_cols(c0)
        xbcext_ref[0:HIST, :] = jnp.zeros((HIST, SSD_CONV_DIM), F32)
        for r0 in row_blocks:
            gate_rows(r0)
            ssd_conv_rows(r0)
        carry_xbc_history(g)
        dt_stage()

    @pl.when(g % tiles_per_seq == 0)
    def _():
        vext_ref[0:HIST, :] = jnp.zeros((HIST, CONV_WIDTH), F32)
        s_ref[...] = jnp.zeros(s_ref.shape, F32)

    def short_conv_rows(r0):
        acc = scw_ref[SHORT_K - 1:SHORT_K, :] * vext_ref[HIST + r0:HIST + r0 + rb, :]
        for k in range(1, SHORT_K):
            acc = acc + scw_ref[SHORT_K - 1 - k:SHORT_K - k, :] * vext_ref[HIST + r0 - k:HIST + r0 - k + rb, :]
        ymix_ref[r0:r0 + rb, 0:CONV_WIDTH] = (pj_ref[r0:r0 + rb, COL_UB:COL_UB + CONV_WIDTH] * acc).astype(BF16)

    row_i = lax.broadcasted_iota(jnp.int32, (2 * CHUNK, LANES), 0)
    col_i = lax.broadcasted_iota(jnp.int32, (2 * CHUNK, LANES), 1)
    same_half = (row_i >= CHUNK) == (col_i >= CHUNK)
    tril2 = jnp.where(same_half & (col_i <= row_i), 1.0, 0.0).astype(F32)
    sub_i = lax.broadcasted_iota(jnp.int32, (CHUNK, LANES), 0)
    lane_i = lax.broadcasted_iota(jnp.int32, (CHUNK, LANES), 1)
    causal2 = (lane_i & (CHUNK - 1)) <= sub_i
    left_half = lane_i < CHUNK

    def pair_cols(zmat, j):
        even = jnp.broadcast_to(zmat[0:CHUNK, j:j + 1], (CHUNK, LANES))
        odd = jnp.broadcast_to(zmat[CHUNK:2 * CHUNK, j:j + 1], (CHUNK, LANES))
        return jnp.where(left_half, even, odd)

    def chunk_prep(c):
        rows = slice(c * CHUNK, (c + 1) * CHUNK)
        a_z = jnp.concatenate([a_ref[rows, 0:LANES], a_ref[rows, LANES:2 * LANES]], axis=0)
        dt_z = jnp.concatenate([dt_ref[rows, 0:LANES], dt_ref[rows, LANES:2 * LANES]], axis=0)
        acum_z = jnp.dot(tril2, a_z, precision=lax.Precision.HIGHEST, preferred_element_type=F32)
        return rows, acum_z, dt_z, acum_z.T, dt_z.T

    def chunk_group(prep, grp):
        rows, acum_z, dt_z, acum_r, dt_r = prep
        c_g = xbc_ref[rows, XBC_C + grp * SSD_STATE:XBC_C + (grp + 1) * SSD_STATE]
        b_g = xbc_ref[rows, XBC_B + grp * SSD_STATE:XBC_B + (grp + 1) * SSD_STATE]
        c_bf = c_g.astype(BF16)
        b2 = jnp.concatenate([b_g, b_g], axis=0)
        g2 = lax.dot_general(c_bf, b2.astype(BF16), (((1,), (1,)), ((), ())),
                             preferred_element_type=F32)
        b_t = b2.T[:, 0:CHUNK].astype(BF16)
        s_g = s_ref[grp]
        cs = jnp.dot(c_bf, s_g.astype(BF16), preferred_element_type=F32)
        pairs = [grp * PAIRS_PER_GROUP + jj for jj in range(PAIRS_PER_GROUP)]
        col_a = [pair_cols(acum_z, j) for j in pairs]
        col_dt = [pair_cols(dt_z, j) for j in pairs]
        xs2 = [xbc_ref[rows, j * LANES:(j + 1) * LANES] for j in pairs]
        y_diag = []
        for jj, j in enumerate(pairs):
            decay = jnp.exp(jnp.where(causal2, col_a[jj] - acum_r[j:j + 1, :], -jnp.inf))
            m2 = (g2 * decay * dt_r[j:j + 1, :]).astype(BF16)
            xx = jnp.where(same_half, jnp.concatenate([xs2[jj], xs2[jj]], axis=0), 0.0).astype(BF16)
            y_diag.append(jnp.dot(m2, xx, preferred_element_type=F32))
        xd, chunk_decay = [], []
        for jj, j in enumerate(pairs):
            cols = slice(j * LANES, (j + 1) * LANES)
            y = y_diag[jj] + cs[:, jj * LANES:(jj + 1) * LANES] * jnp.exp(col_a[jj])
            y = y + dskip_ref[:, cols] * xs2[jj]
            ybuf_ref[rows, cols] = y * zs_ref[rows, cols]
            last_a = col_a[jj][CHUNK - 1:CHUNK, :]
            xd.append((xs2[jj] * col_dt[jj] * jnp.exp(last_a - col_a[jj])).astype(BF16))
            chunk_decay.append(jnp.exp(last_a))
        ds = jnp.dot(b_t, jnp.concatenate(xd, axis=1), preferred_element_type=F32)
        s_ref[grp] = s_g * jnp.concatenate(chunk_decay, axis=1) + ds

    def group_norm_rows(r0):
        for grp in range(SSD_GROUPS):
            gc = slice(grp * GROUP_COLS, (grp + 1) * GROUP_COLS)
            yg = _rms(ybuf_ref[r0:r0 + rb, gc], gnw_ref[:, gc])
            ymix_ref[r0:r0 + rb, CONV_WIDTH + grp * GROUP_COLS:CONV_WIDTH + (grp + 1) * GROUP_COLS] = (
                yg.astype(BF16))

    pending = [c0 for c0 in proj_blocks if c0 >= COL_Z] + [c0 for c0 in proj_blocks if c0 < COL_Z]

    def project_next(n):
        for _ in range(n):
            if pending:
                project_cols(pending.pop(0))

    half = len(row_blocks) // 2
    for r0 in row_blocks:
        norm_rows(xn_ref, r0)
    project_next(1)
    for r0 in row_blocks:
        vext_ref[HIST + r0:HIST + r0 + rb, :] = (
            pj_ref[r0:r0 + rb, COL_UC:COL_UC + CONV_WIDTH] * pj_ref[r0:r0 + rb, COL_UH:COL_UH + CONV_WIDTH])
    project_next(1)
    for r0 in row_blocks[:half]:
        short_conv_rows(r0)
    project_next(1)
    for r0 in row_blocks[half:]:
        short_conv_rows(r0)
    vext_ref[0:HIST, :] = vext_ref[tt:tt + HIST, :]
    n_chunks = tt // CHUNK
    for c in range(n_chunks):
        prep = chunk_prep(c)
        for grp in range(SSD_GROUPS):
            project_next(1)
            chunk_group(prep, grp)
    project_next(len(pending))
    for r0 in row_blocks:
        group_norm_rows(r0)
    ahead = [functools.partial(gate_rows, r0) for r0 in row_blocks]
    ahead += [functools.partial(ssd_conv_rows, r0) for r0 in row_blocks]
    out_blocks = list(range(0, D_MODEL, nc))
    per_block = -(-len(ahead) // len(out_blocks))
    for n0 in out_blocks:
        o_ref[:, n0:n0 + nc] = x_ref[:, n0:n0 + nc] + jnp.dot(
            ymix_ref[...], wout_ref[:, n0:n0 + nc], preferred_element_type=F32)
        for stage in ahead[:per_block]:
            stage()
        del ahead[:per_block]
    carry_xbc_history(g + 1)
    dt_stage()


def _mlp_kernel(x_ref, nw_ref, wup_ref, wdn_ref, fw_ref, win32_ref, wout32_ref, o_ref, win16_ref, wout16_ref,
                h_ref, hid_ref, *, tm, nc, rb, final_norm):
    win16_ref[...] = win32_ref[...].T.astype(BF16)
    wout16_ref[...] = wout32_ref[...].astype(BF16)
    for r0 in range(0, tm, rb):
        h_ref[r0:r0 + rb, :] = _rms(x_ref[r0:r0 + rb, :], nw_ref[...]).astype(BF16)
    for f0 in range(0, D_FF, nc):
        u = jnp.maximum(jnp.dot(h_ref[...], wup_ref[:, f0:f0 + nc], preferred_element_type=F32), 0.0)
        hid_ref[:, f0:f0 + nc] = (u * u).astype(BF16)
    for n0 in range(0, D_MODEL, nc):
        o_ref[:, n0:n0 + nc] = x_ref[:, n0:n0 + nc] + jnp.dot(
            hid_ref[...], wdn_ref[:, n0:n0 + nc], preferred_element_type=F32)
    if final_norm:
        for r0 in range(0, tm, rb):
            o_ref[r0:r0 + rb, :] = _rms(o_ref[r0:r0 + rb, :], fw_ref[...])


def _resident(shape, layer):
    nd = len(shape)
    return pl.BlockSpec((None,) + tuple(shape), lambda *_: (layer,) + (0,) * nd,
                        pipeline_mode=pl.Buffered(1))


def _whole(shape):
    return pl.BlockSpec(tuple(shape), lambda *_: (0,) * len(shape), pipeline_mode=pl.Buffered(1))


def _cast_specs(shape, steps, layer):
    rows, cols = shape
    n_blocks = 1
    while n_blocks * 2 <= min(steps, rows // BF16_ROWS):
        n_blocks *= 2
    assert rows % n_blocks == 0 and (rows // n_blocks) % BF16_ROWS == 0
    br = rows // n_blocks
    src = pl.BlockSpec((None, br, cols), lambda i: (layer, jnp.minimum(i, n_blocks - 1), 0))
    dst = pl.BlockSpec((br, cols), lambda i: (jnp.minimum(i, n_blocks - 1), 0))
    return src, dst


def _mixer_call(x2, p, w_main16, w_out16, w_up, w_down, layer, tt, tiles_per_seq, vmem_limit):
    n, d = x2.shape
    n_tiles = n // tt
    kern = functools.partial(_mixer_kernel, tt=tt, nc=512, rb=32, tiles_per_seq=tiles_per_seq)
    in_specs = [
        pl.BlockSpec((tt, d), lambda g: (g, 0)),
        pl.BlockSpec((tt, d), lambda g: (jnp.minimum(g + 1, n_tiles - 1), 0)),
        _resident((1, d), layer),
        pl.BlockSpec((d, MAIN_COLS), lambda g: (0, 0), pipeline_mode=pl.Buffered(1)),
        _resident((d, 2 * LANES), layer),
        _resident((HIST, CONV_WIDTH), layer),
        _resident((HIST, SSD_CONV_DIM), layer),
        _resident((1, SSD_CONV_DIM), layer),
        _resident((1, 2 * LANES), layer),
        _resident((1, 2 * LANES), layer),
        _resident((1, SSD_INNER), layer),
        _resident((1, SSD_INNER), layer),
        _whole((MIX_WIDTH, d)),
    ]
    scratch = [
        pltpu.VMEM((tt, d), BF16),
        pltpu.VMEM((tt, COL_XBC), F32),
        pltpu.VMEM((tt + HIST, CONV_WIDTH), F32),
        pltpu.VMEM((tt + HIST, SSD_CONV_DIM), F32),
        pltpu.VMEM((tt, 2 * LANES), F32),
        pltpu.VMEM((tt, SSD_CONV_DIM), F32),
        pltpu.VMEM((tt, SSD_INNER), F32),
        pltpu.VMEM((tt, 2 * LANES), F32),
        pltpu.VMEM((tt, 2 * LANES), F32),
        pltpu.VMEM((tt, SSD_INNER), F32),
        pltpu.VMEM((tt, MIX_WIDTH), BF16),
        pltpu.VMEM((SSD_GROUPS, SSD_STATE, GROUP_COLS), F32),
    ]
    up_in, up_out = _cast_specs(w_up.shape[1:], n_tiles, layer)
    dn_in, dn_out = _cast_specs(w_down.shape[1:], n_tiles, layer)
    return pl.pallas_call(
        kern,
        out_shape=(jax.ShapeDtypeStruct(x2.shape, x2.dtype),
                   jax.ShapeDtypeStruct(w_up.shape[1:], BF16), jax.ShapeDtypeStruct(w_down.shape[1:], BF16)),
        grid=(n_tiles,),
        in_specs=in_specs + [up_in, dn_in],
        out_specs=(pl.BlockSpec((tt, d), lambda g: (g, 0)), up_out, dn_out),
        scratch_shapes=scratch,
        compiler_params=pltpu.CompilerParams(
            dimension_semantics=("arbitrary",), vmem_limit_bytes=vmem_limit),
        name=f"mixer_l{layer}",
    )(x2, x2, p["norm_mix_w"], w_main16, p["w_dt"], p["short_conv_w"], p["ssd_conv_w"], p["ssd_conv_b"],
      p["dt_bias"], p["a_log"], p["d_skip"], p["ssd_norm_w"], w_out16, w_up, w_down)


def _mlp_call(x2, p, w_up16, w_down16, w_in_t, w_out, layer, next_layer, tm, vmem_limit, final_norm):
    n, d = x2.shape
    steps = n // tm
    kern = functools.partial(_mlp_kernel, tm=tm, nc=512, rb=32, final_norm=final_norm)
    lane_tiles = MAIN_COLS // LANES
    n_blocks = max(k for k in range(1, lane_tiles + 1) if lane_tiles % k == 0 and k <= steps)
    bc = MAIN_COLS // n_blocks
    in_in = pl.BlockSpec((None, bc, d), lambda i: (next_layer, jnp.minimum(i, n_blocks - 1), 0))
    in_out = pl.BlockSpec((d, bc), lambda i: (0, jnp.minimum(i, n_blocks - 1)))
    out_in, out_out = _cast_specs(w_out.shape[1:], steps, next_layer)
    return pl.pallas_call(
        kern,
        out_shape=(jax.ShapeDtypeStruct(x2.shape, x2.dtype),
                   jax.ShapeDtypeStruct((d, MAIN_COLS), BF16), jax.ShapeDtypeStruct(w_out.shape[1:], BF16)),
        grid=(steps,),
        in_specs=[
            pl.BlockSpec((tm, d), lambda i: (i, 0)),
            _resident((1, d), layer),
            _whole((d, D_FF)),
            _whole((D_FF, d)),
            pl.BlockSpec((1, d), lambda i: (0, 0)),
            in_in, out_in,
        ],
        out_specs=(pl.BlockSpec((tm, d), lambda i: (i, 0)), in_out, out_out),
        scratch_shapes=[pltpu.VMEM((tm, d), BF16), pltpu.VMEM((tm, D_FF), BF16)],
        compiler_params=pltpu.CompilerParams(
            dimension_semantics=("arbitrary",), vmem_limit_bytes=vmem_limit),
        name=f"mlp_l{layer}",
    )(x2, p["norm_mlp_w"], w_up16, w_down16, p["final_norm_w"], w_in_t, w_out)


def _even_odd_lanes(v):
    pad = [(0, 0)] * (v.ndim - 1) + [(0, LANES - PAIRS)]
    return jnp.concatenate([jnp.pad(v[..., 0::2], pad), jnp.pad(v[..., 1::2], pad)], axis=-1)


def _pad_rows(w, rows):
    return jnp.pad(w, ((0, 0), (0, rows - w.shape[1]), (0, 0)))


def kernel(x, norm_mix_w, w_in, short_conv_w, ssd_conv_w, ssd_conv_b, dt_bias, a_log, d_skip, ssd_norm_w,
           w_out, norm_mlp_w, w_up, w_down, final_norm_w):
    depth = w_in.shape[0]
    b, t, d = x.shape
    tt = min(256, t)
    tm = min(512, b * t)
    assert d == D_MODEL and t % tt == 0 and tt % CHUNK == 0 and (b * t) % tm == 0
    vmem_limit = 56 * 1024 * 1024
    p = {
        "norm_mix_w": norm_mix_w[:, None, :],
        "w_dt": _even_odd_lanes(w_in[:, :, MAIN_COLS:]).astype(BF16),
        "short_conv_w": _pad_rows(short_conv_w, HIST),
        "ssd_conv_w": _pad_rows(ssd_conv_w, HIST),
        "ssd_conv_b": ssd_conv_b[:, None, :],
        "dt_bias": _even_odd_lanes(dt_bias)[:, None, :],
        "a_log": _even_odd_lanes(a_log)[:, None, :],
        "d_skip": jnp.repeat(d_skip, SSD_HEAD_DIM, axis=-1)[:, None, :],
        "ssd_norm_w": ssd_norm_w[:, None, :],
        "norm_mlp_w": norm_mlp_w[:, None, :],
        "final_norm_w": final_norm_w[None, :],
    }
    x2 = x.reshape(b * t, d)
    w_main16, w_out16 = w_in[0, :, :MAIN_COLS].astype(BF16), w_out[0].astype(BF16)
    w_in_t = jnp.swapaxes(w_in, 1, 2)
    for layer in range(depth):
        x2, w_up16, w_down16 = _mixer_call(x2, p, w_main16, w_out16, w_up, w_down, layer, tt, t // tt, vmem_limit)
        x2, w_main16, w_out16 = _mlp_call(x2, p, w_up16, w_down16, w_in_t, w_out, layer, (layer + 1) % depth, tm,
                                          vmem_limit, layer == depth - 1)
    return x2.reshape(b, t, d)
```

```python
import functools

import jax
import jax.numpy as jnp
from jax import lax
from jax.experimental import pallas as pl
from jax.experimental.pallas import tpu as pltpu

D_MODEL = 1024
CHUNK = 64
EPS = 1e-5
CONV_WIDTH = 1024
SHORT_K = 3
SSD_HEAD_DIM = 64
SSD_HEADS = 16
SSD_INNER = 1024
SSD_GROUPS = 2
SSD_STATE = 128
SSD_CONV_K = 4
SSD_CONV_DIM = SSD_INNER + 2 * SSD_GROUPS * SSD_STATE
MIX_WIDTH = CONV_WIDTH + SSD_INNER
D_FF = 4 * D_MODEL
MAIN_COLS = 3 * CONV_WIDTH + SSD_INNER + SSD_CONV_DIM
COL_UB, COL_UC, COL_UH, COL_Z, COL_XBC = 0, 1024, 2048, 3072, 4096
XBC_B = SSD_INNER
XBC_C = SSD_INNER + SSD_GROUPS * SSD_STATE
LANES = 128
HIST = 8
BF16_ROWS = 16
PAIRS = SSD_HEADS // 2
PAIRS_PER_GROUP = PAIRS // SSD_GROUPS
GROUP_COLS = SSD_INNER // SSD_GROUPS

F32 = jnp.float32
BF16 = jnp.bfloat16


def _rms(x, w):
    ms = jnp.mean(x * x, axis=-1, keepdims=True)
    return x * lax.rsqrt(ms + EPS) * w


def _softplus(x):
    return jnp.maximum(x, 0.0) + jnp.log1p(jnp.exp(-jnp.abs(x)))


def _silu(x):
    return x * jax.nn.sigmoid(x)


def _mixer_kernel(x_ref, xn_ref, nw_ref, wmain_ref, wdt_ref, scw_ref, ssw_ref, ssb_ref, dtb_ref, alog_ref,
                  dskip_ref, gnw_ref, wout_ref, wup32_ref, wdn32_ref, o_ref, wup16_ref, wdn16_ref,
                  h_ref, pj_ref, vext_ref, xbcext_ref, dtraw_ref, xbc_ref, zs_ref, dt_ref, a_ref, ybuf_ref,
                  ymix_ref, s_ref, *, tt, nc, rb, tiles_per_seq):
    g = pl.program_id(0)

    wup16_ref[...] = wup32_ref[...].astype(BF16)
    wdn16_ref[...] = wdn32_ref[...].astype(BF16)

    def norm_rows(src_ref, r0):
        h_ref[r0:r0 + rb, :] = _rms(src_ref[r0:r0 + rb, :], nw_ref[...]).astype(BF16)

    def project_cols(c0):
        if c0 < COL_XBC:
            pj_ref[:, c0:c0 + nc] = jnp.dot(h_ref[...], wmain_ref[:, c0:c0 + nc], preferred_element_type=F32)
        elif c0 < MAIN_COLS:
            xbcext_ref[HIST:HIST + tt, c0 - COL_XBC:c0 - COL_XBC + nc] = jnp.dot(
                h_ref[...], wmain_ref[:, c0:c0 + nc], preferred_element_type=F32)
        else:
            dtraw_ref[...] = jnp.dot(h_ref[...], wdt_ref[...], preferred_element_type=F32)

    proj_blocks = list(range(0, MAIN_COLS, nc)) + [MAIN_COLS]
    row_blocks = list(range(0, tt, rb))

    def gate_rows(r0):
        zs_ref[r0:r0 + rb, :] = _silu(pj_ref[r0:r0 + rb, COL_Z:COL_Z + SSD_INNER])

    def ssd_conv_rows(r0):
        acc = ssb_ref[...] + ssw_ref[SSD_CONV_K - 1:SSD_CONV_K, :] * xbcext_ref[HIST + r0:HIST + r0 + rb, :]
        for k in range(1, SSD_CONV_K):
            acc = acc + (ssw_ref[SSD_CONV_K - 1 - k:SSD_CONV_K - k, :]
                         * xbcext_ref[HIST + r0 - k:HIST + r0 - k + rb, :])
        xbc_ref[r0:r0 + rb, :] = _silu(acc)

    def carry_xbc_history(tile):
        next_starts_seq = (tile + 1) % tiles_per_seq == 0
        xbcext_ref[0:HIST, :] = jnp.where(next_starts_seq, 0.0, xbcext_ref[tt:tt + HIST, :])

    def dt_stage():
        dt = _softplus(dtraw_ref[...] + dtb_ref[...])
        dt_ref[...] = dt
        a_ref[...] = dt * (-jnp.exp(alog_ref[...]))

    @pl.when(g == 0)
    def _():
        for r0 in row_blocks:
            norm_rows(x_ref, r0)
        for c0 in proj_blocks:
            project_cols(c0)
        xbcext_ref[0:HIST, :] = jnp.zeros((HIST, SSD_CONV_DIM), F32)
        for r0 in row_blocks:
            gate_rows(r0)
            ssd_conv_rows(r0)
        carry_xbc_history(g)
        dt_stage()

    @pl.when(g % tiles_per_seq == 0)
    def _():
        vext_ref[0:HIST, :] = jnp.zeros((HIST, CONV_WIDTH), F32)
        s_ref[...] = jnp.zeros(s_ref.shape, F32)

    def short_conv_rows(r0):
        acc = scw_ref[SHORT_K - 1:SHORT_K, :] * vext_ref[HIST + r0:HIST + r0 + rb, :]
        for k in range(1, SHORT_K):
            acc = acc + scw_ref[SHORT_K - 1 - k:SHORT_K - k, :] * vext_ref[HIST + r0 - k:HIST + r0 - k + rb, :]
        ymix_ref[r0:r0 + rb, 0:CONV_WIDTH] = (pj_ref[r0:r0 + rb, COL_UB:COL_UB + CONV_WIDTH] * acc).astype(BF16)

    row_i = lax.broadcasted_iota(jnp.int32, (2 * CHUNK, LANES), 0)
    col_i = lax.broadcasted_iota(jnp.int32, (2 * CHUNK, LANES), 1)
    same_half = (row_i >= CHUNK) == (col_i >= CHUNK)
    tril2 = jnp.where(same_half & (col_i <= row_i), 1.0, 0.0).astype(F32)
    sub_i = lax.broadcasted_iota(jnp.int32, (CHUNK, LANES), 0)
    lane_i = lax.broadcasted_iota(jnp.int32, (CHUNK, LANES), 1)
    causal2 = (lane_i & (CHUNK - 1)) <= sub_i
    left_half = lane_i < CHUNK

    def pair_cols(zmat, j):
        even = jnp.broadcast_to(zmat[0:CHUNK, j:j + 1], (CHUNK, LANES))
        odd = jnp.broadcast_to(zmat[CHUNK:2 * CHUNK, j:j + 1], (CHUNK, LANES))
        return jnp.where(left_half, even, odd)

    def chunk_prep(c):
        rows = slice(c * CHUNK, (c + 1) * CHUNK)
        a_z = jnp.concatenate([a_ref[rows, 0:LANES], a_ref[rows, LANES:2 * LANES]], axis=0)
        dt_z = jnp.concatenate([dt_ref[rows, 0:LANES], dt_ref[rows, LANES:2 * LANES]], axis=0)
        acum_z = jnp.dot(tril2, a_z, precision=lax.Precision.HIGHEST, preferred_element_type=F32)
        return rows, acum_z, dt_z, acum_z.T, dt_z.T

    def chunk_group(prep, grp):
        rows, acum_z, dt_z, acum_r, dt_r = prep
        c_g = xbc_ref[rows, XBC_C + grp * SSD_STATE:XBC_C + (grp + 1) * SSD_STATE]
        b_g = xbc_ref[rows, XBC_B + grp * SSD_STATE:XBC_B + (grp + 1) * SSD_STATE]
        c_bf = c_g.astype(BF16)
        b2 = jnp.concatenate([b_g, b_g], axis=0)
        g2 = lax.dot_general(c_bf, b2.astype(BF16), (((1,), (1,)), ((), ())),
                             preferred_element_type=F32)
        b_t = b2.T[:, 0:CHUNK].astype(BF16)
        s_g = s_ref[grp]
        cs = jnp.dot(c_bf, s_g.astype(BF16), preferred_element_type=F32)
        pairs = [grp * PAIRS_PER_GROUP + jj for jj in range(PAIRS_PER_GROUP)]
        col_a = [pair_cols(acum_z, j) for j in pairs]
        col_dt = [pair_cols(dt_z, j) for j in pairs]
        xs2 = [xbc_ref[rows, j * LANES:(j + 1) * LANES] for j in pairs]
        y_diag = []
        for jj, j in enumerate(pairs):
            decay = jnp.exp(jnp.where(causal2, col_a[jj] - acum_r[j:j + 1, :], -jnp.inf))
            m2 = (g2 * decay * dt_r[j:j + 1, :]).astype(BF16)
            xx = jnp.where(same_half, jnp.concatenate([xs2[jj], xs2[jj]], axis=0), 0.0).astype(BF16)
            y_diag.append(jnp.dot(m2, xx, preferred_element_type=F32))
        xd, chunk_decay = [], []
        for jj, j in enumerate(pairs):
            cols = slice(j * LANES, (j + 1) * LANES)
            y = y_diag[jj] + cs[:, jj * LANES:(jj + 1) * LANES] * jnp.exp(col_a[jj])
            y = y + dskip_ref[:, cols] * xs2[jj]
            ybuf_ref[rows, cols] = y * zs_ref[rows, cols]
            last_a = col_a[jj][CHUNK - 1:CHUNK, :]
            xd.append((xs2[jj] * col_dt[jj] * jnp.exp(last_a - col_a[jj])).astype(BF16))
            chunk_decay.append(jnp.exp(last_a))
        ds = jnp.dot(b_t, jnp.concatenate(xd, axis=1), preferred_element_type=F32)
        s_ref[grp] = s_g * jnp.concatenate(chunk_decay, axis=1) + ds

    def group_norm_rows(r0):
        for grp in range(SSD_GROUPS):
            gc = slice(grp * GROUP_COLS, (grp + 1) * GROUP_COLS)
            yg = _rms(ybuf_ref[r0:r0 + rb, gc], gnw_ref[:, gc])
            ymix_ref[r0:r0 + rb, CONV_WIDTH + grp * GROUP_COLS:CONV_WIDTH + (grp + 1) * GROUP_COLS] = (
                yg.astype(BF16))

    pending = [c0 for c0 in proj_blocks if c0 >= COL_Z] + [c0 for c0 in proj_blocks if c0 < COL_Z]

    def project_next(n):
        for _ in range(n):
            if pending:
                project_cols(pending.pop(0))

    half = len(row_blocks) // 2
    for r0 in row_blocks:
        norm_rows(xn_ref, r0)
    project_next(1)
    for r0 in row_blocks:
        vext_ref[HIST + r0:HIST + r0 + rb, :] = (
            pj_ref[r0:r0 + rb, COL_UC:COL_UC + CONV_WIDTH] * pj_ref[r0:r0 + rb, COL_UH:COL_UH + CONV_WIDTH])
    project_next(1)
    for r0 in row_blocks[:half]:
        short_conv_rows(r0)
    project_next(1)
    for r0 in row_blocks[half:]:
        short_conv_rows(r0)
    vext_ref[0:HIST, :] = vext_ref[tt:tt + HIST, :]
    n_chunks = tt // CHUNK
    for c in range(n_chunks):
        prep = chunk_prep(c)
        for grp in range(SSD_GROUPS):
            project_next(1)
            chunk_group(prep, grp)
    project_next(len(pending))
    for r0 in row_blocks:
        group_norm_rows(r0)
    ahead = [functools.partial(gate_rows, r0) for r0 in row_blocks]
    ahead += [functools.partial(ssd_conv_rows, r0) for r0 in row_blocks]
    out_blocks = list(range(0, D_MODEL, nc))
    per_block = -(-len(ahead) // len(out_blocks))
    for n0 in out_blocks:
        o_ref[:, n0:n0 + nc] = x_ref[:, n0:n0 + nc] + jnp.dot(
            ymix_ref[...], wout_ref[:, n0:n0 + nc], preferred_element_type=F32)
        for stage in ahead[:per_block]:
            stage()
        del ahead[:per_block]
    carry_xbc_history(g + 1)
    dt_stage()


def _mlp_kernel(x_ref, nw_ref, wup_ref, wdn_ref, fw_ref, win32_ref, wout32_ref, o_ref, win16_ref, wout16_ref,
                h_ref, hid_ref, *, tm, nc, rb, final_norm):
    win16_ref[...] = win32_ref[...].T.astype(BF16)
    wout16_ref[...] = wout32_ref[...].astype(BF16)
    for r0 in range(0, tm, rb):
        h_ref[r0:r0 + rb, :] = _rms(x_ref[r0:r0 + rb, :], nw_ref[...]).astype(BF16)
    for f0 in range(0, D_FF, nc):
        u = jnp.maximum(jnp.dot(h_ref[...], wup_ref[:, f0:f0 + nc], preferred_element_type=F32), 0.0)
        hid_ref[:, f0:f0 + nc] = (u * u).astype(BF16)
    for n0 in range(0, D_MODEL, nc):
        o_ref[:, n0:n0 + nc] = x_ref[:, n0:n0 + nc] + jnp.dot(
            hid_ref[...], wdn_ref[:, n0:n0 + nc], preferred_element_type=F32)
    if final_norm:
        for r0 in range(0, tm, rb):
            o_ref[r0:r0 + rb, :] = _rms(o_ref[r0:r0 + rb, :], fw_ref[...])


def _resident(shape, layer):
    nd = len(shape)
    return pl.BlockSpec((None,) + tuple(shape), lambda *_: (layer,) + (0,) * nd,
                        pipeline_mode=pl.Buffered(1))


def _whole(shape):
    return pl.BlockSpec(tuple(shape), lambda *_: (0,) * len(shape), pipeline_mode=pl.Buffered(1))


def _cast_specs(shape, steps, layer):
    rows, cols = shape
    n_blocks = 1
    while n_blocks * 2 <= min(steps, rows // BF16_ROWS):
        n_blocks *= 2
    assert rows % n_blocks == 0 and (rows // n_blocks) % BF16_ROWS == 0
    br = rows // n_blocks
    src = pl.BlockSpec((None, br, cols), lambda i: (layer, jnp.minimum(i, n_blocks - 1), 0))
    dst = pl.BlockSpec((br, cols), lambda i: (jnp.minimum(i, n_blocks - 1), 0))
    return src, dst


def _mixer_call(x2, p, w_main16, w_out16, w_up, w_down, layer, tt, tiles_per_seq, vmem_limit):
    n, d = x2.shape
    n_tiles = n // tt
    kern = functools.partial(_mixer_kernel, tt=tt, nc=512, rb=32, tiles_per_seq=tiles_per_seq)
    in_specs = [
        pl.BlockSpec((tt, d), lambda g: (g, 0)),
        pl.BlockSpec((tt, d), lambda g: (jnp.minimum(g + 1, n_tiles - 1), 0)),
        _resident((1, d), layer),
        pl.BlockSpec((d, MAIN_COLS), lambda g: (0, 0), pipeline_mode=pl.Buffered(1)),
        _resident((d, 2 * LANES), layer),
        _resident((HIST, CONV_WIDTH), layer),
        _resident((HIST, SSD_CONV_DIM), layer),
        _resident((1, SSD_CONV_DIM), layer),
        _resident((1, 2 * LANES), layer),
        _resident((1, 2 * LANES), layer),
        _resident((1, SSD_INNER), layer),
        _resident((1, SSD_INNER), layer),
        _whole((MIX_WIDTH, d)),
    ]
    scratch = [
        pltpu.VMEM((tt, d), BF16),
        pltpu.VMEM((tt, COL_XBC), F32),
        pltpu.VMEM((tt + HIST, CONV_WIDTH), F32),
        pltpu.VMEM((tt + HIST, SSD_CONV_DIM), F32),
        pltpu.VMEM((tt, 2 * LANES), F32),
        pltpu.VMEM((tt, SSD_CONV_DIM), F32),
        pltpu.VMEM((tt, SSD_INNER), F32),
        pltpu.VMEM((tt, 2 * LANES), F32),
        pltpu.VMEM((tt, 2 * LANES), F32),
        pltpu.VMEM((tt, SSD_INNER), F32),
        pltpu.VMEM((tt, MIX_WIDTH), BF16),
        pltpu.VMEM((SSD_GROUPS, SSD_STATE, GROUP_COLS), F32),
    ]
    up_in, up_out = _cast_specs(w_up.shape[1:], n_tiles, layer)
    dn_in, dn_out = _cast_specs(w_down.shape[1:], n_tiles, layer)
    return pl.pallas_call(
        kern,
        out_shape=(jax.ShapeDtypeStruct(x2.shape, x2.dtype),
                   jax.ShapeDtypeStruct(w_up.shape[1:], BF16), jax.ShapeDtypeStruct(w_down.shape[1:], BF16)),
        grid=(n_tiles,),
        in_specs=in_specs + [up_in, dn_in],
        out_specs=(pl.BlockSpec((tt, d), lambda g: (g, 0)), up_out, dn_out),
        scratch_shapes=scratch,
        compiler_params=pltpu.CompilerParams(
            dimension_semantics=("arbitrary",), vmem_limit_bytes=vmem_limit),
        name=f"mixer_l{layer}",
    )(x2, x2, p["norm_mix_w"], w_main16, p["w_dt"], p["short_conv_w"], p["ssd_conv_w"], p["ssd_conv_b"],
      p["dt_bias"], p["a_log"], p["d_skip"], p["ssd_norm_w"], w_out16, w_up, w_down)


def _mlp_call(x2, p, w_up16, w_down16, w_in_t, w_out, layer, next_layer, tm, vmem_limit, final_norm):
    n, d = x2.shape
    steps = n // tm
    kern = functools.partial(_mlp_kernel, tm=tm, nc=512, rb=32, final_norm=final_norm)
    lane_tiles = MAIN_COLS // LANES
    n_blocks = max(k for k in range(1, lane_tiles + 1) if lane_tiles % k == 0 and k <= steps)
    bc = MAIN_COLS // n_blocks
    in_in = pl.BlockSpec((None, bc, d), lambda i: (next_layer, jnp.minimum(i, n_blocks - 1), 0))
    in_out = pl.BlockSpec((d, bc), lambda i: (0, jnp.minimum(i, n_blocks - 1)))
    out_in, out_out = _cast_specs(w_out.shape[1:], steps, next_layer)
    return pl.pallas_call(
        kern,
        out_shape=(jax.ShapeDtypeStruct(x2.shape, x2.dtype),
                   jax.ShapeDtypeStruct((d, MAIN_COLS), BF16), jax.ShapeDtypeStruct(w_out.shape[1:], BF16)),
        grid=(steps,),
        in_specs=[
            pl.BlockSpec((tm, d), lambda i: (i, 0)),
            _resident((1, d), layer),
            _whole((d, D_FF)),
            _whole((D_FF, d)),
            pl.BlockSpec((1, d), lambda i: (0, 0)),
            in_in, out_in,
        ],
        out_specs=(pl.BlockSpec((tm, d), lambda i: (i, 0)), in_out, out_out),
        scratch_shapes=[pltpu.VMEM((tm, d), BF16), pltpu.VMEM((tm, D_FF), BF16)],
        compiler_params=pltpu.CompilerParams(
            dimension_semantics=("arbitrary",), vmem_limit_bytes=vmem_limit),
        name=f"mlp_l{layer}",
    )(x2, p["norm_mlp_w"], w_up16, w_down16, p["final_norm_w"], w_in_t, w_out)


def _even_odd_lanes(v):
    pad = [(0, 0)] * (v.ndim - 1) + [(0, LANES - PAIRS)]
    return jnp.concatenate([jnp.pad(v[..., 0::2], pad), jnp.pad(v[..., 1::2], pad)], axis=-1)


def _pad_rows(w, rows):
    return jnp.pad(w, ((0, 0), (0, rows - w.shape[1]), (0, 0)))


def kernel(x, norm_mix_w, w_in, short_conv_w, ssd_conv_w, ssd_conv_b, dt_bias, a_log, d_skip, ssd_norm_w,
           w_out, norm_mlp_w, w_up, w_down, final_norm_w):
    depth = w_in.shape[0]
    b, t, d = x.shape
    tt = min(256, t)
    tm = min(1024, b * t)
    assert d == D_MODEL and t % tt == 0 and tt % CHUNK == 0 and (b * t) % tm == 0
    vmem_limit = 56 * 1024 * 1024
    p = {
        "norm_mix_w": norm_mix_w[:, None, :],
        "w_dt": _even_odd_lanes(w_in[:, :, MAIN_COLS:]).astype(BF16),
        "short_conv_w": _pad_rows(short_conv_w, HIST),
        "ssd_conv_w": _pad_rows(ssd_conv_w, HIST),
        "ssd_conv_b": ssd_conv_b[:, None, :],
        "dt_bias": _even_odd_lanes(dt_bias)[:, None, :],
        "a_log": _even_odd_lanes(a_log)[:, None, :],
        "d_skip": jnp.repeat(d_skip, SSD_HEAD_DIM, axis=-1)[:, None, :],
        "ssd_norm_w": ssd_norm_w[:, None, :],
        "norm_mlp_w": norm_mlp_w[:, None, :],
        "final_norm_w": final_norm_w[None, :],
    }
    x2 = x.reshape(b * t, d)
    w_main16, w_out16 = w_in[0, :, :MAIN_COLS].astype(BF16), w_out[0].astype(BF16)
    w_in_t = jnp.swapaxes(w_in, 1, 2)
    for layer in range(depth):
        x2, w_up16, w_down16 = _mixer_call(x2, p, w_main16, w_out16, w_up, w_down, layer, tt, t // tt, vmem_limit)
        x2, w_main16, w_out16 = _mlp_call(x2, p, w_up16, w_down16, w_in_t, w_out, layer, (layer + 1) % depth, tm,
                                          vmem_limit, layer == depth - 1)
    return x2.reshape(b, t, d)
```

```python
import functools

import jax
import jax.numpy as jnp
from jax import lax
from jax.experimental import pallas as pl
from jax.experimental.pallas import tpu as pltpu

D_MODEL = 1024
CHUNK = 64
EPS = 1e-5
CONV_WIDTH = 1024
SHORT_K = 3
SSD_HEAD_DIM = 64
SSD_HEADS = 16
SSD_INNER = 1024
SSD_GROUPS = 2
SSD_STATE = 128
SSD_CONV_K = 4
SSD_CONV_DIM = SSD_INNER + 2 * SSD_GROUPS * SSD_STATE
MIX_WIDTH = CONV_WIDTH + SSD_INNER
D_FF = 4 * D_MODEL
MAIN_COLS = 3 * CONV_WIDTH + SSD_INNER + SSD_CONV_DIM
COL_UB, COL_UC, COL_UH, COL_Z, COL_XBC = 0, 1024, 2048, 3072, 4096
XBC_B = SSD_INNER
XBC_C = SSD_INNER + SSD_GROUPS * SSD_STATE
LANES = 128
HIST = 8
BF16_ROWS = 16
PAIRS = SSD_HEADS // 2
PAIRS_PER_GROUP = PAIRS // SSD_GROUPS
GROUP_COLS = SSD_INNER // SSD_GROUPS

F32 = jnp.float32
BF16 = jnp.bfloat16


def _rms(x, w):
    ms = jnp.mean(x * x, axis=-1, keepdims=True)
    return x * lax.rsqrt(ms + EPS) * w


def _softplus(x):
    return jnp.maximum(x, 0.0) + jnp.log1p(jnp.exp(-jnp.abs(x)))


def _silu(x):
    return x * jax.nn.sigmoid(x)


def _mixer_kernel(x_ref, xn_ref, nw_ref, wmain_ref, wdt_ref, scw_ref, ssw_ref, ssb_ref, dtb_ref, alog_ref,
                  dskip_ref, gnw_ref, wout_ref, wup32_ref, wdn32_ref, o_ref, wup16_ref, wdn16_ref,
                  h_ref, pj_ref, vext_ref, xbcext_ref, dtraw_ref, xbc_ref, zs_ref, dt_ref, a_ref, ybuf_ref,
                  ymix_ref, s_ref, *, tt, nc, pc, rb, tiles_per_seq):
    g = pl.program_id(0)

    wup16_ref[...] = wup32_ref[...].astype(BF16)
    wdn16_ref[...] = wdn32_ref[...].astype(BF16)

    def norm_rows(src_ref, r0):
        h_ref[r0:r0 + rb, :] = _rms(src_ref[r0:r0 + rb, :], nw_ref[...]).astype(BF16)

    def project_cols(c0):
        if c0 < COL_XBC:
            pj_ref[:, c0:c0 + pc] = jnp.dot(h_ref[...], wmain_ref[:, c0:c0 + pc], preferred_element_type=F32)
        elif c0 < MAIN_COLS:
            xbcext_ref[HIST:HIST + tt, c0 - COL_XBC:c0 - COL_XBC + pc] = jnp.dot(
                h_ref[...], wmain_ref[:, c0:c0 + pc], preferred_element_type=F32)
        else:
            dtraw_ref[...] = jnp.dot(h_ref[...], wdt_ref[...], preferred_element_type=F32)

    proj_blocks = list(range(0, MAIN_COLS, pc)) + [MAIN_COLS]
    row_blocks = list(range(0, tt, rb))

    def gate_rows(r0):
        zs_ref[r0:r0 + rb, :] = _silu(pj_ref[r0:r0 + rb, COL_Z:COL_Z + SSD_INNER])

    def ssd_conv_rows(r0):
        acc = ssb_ref[...] + ssw_ref[SSD_CONV_K - 1:SSD_CONV_K, :] * xbcext_ref[HIST + r0:HIST + r0 + rb, :]
        for k in range(1, SSD_CONV_K):
            acc = acc + (ssw_ref[SSD_CONV_K - 1 - k:SSD_CONV_K - k, :]
                         * xbcext_ref[HIST + r0 - k:HIST + r0 - k + rb, :])
        xbc_ref[r0:r0 + rb, :] = _silu(acc)

    def carry_xbc_history(tile):
        next_starts_seq = (tile + 1) % tiles_per_seq == 0
        xbcext_ref[0:HIST, :] = jnp.where(next_starts_seq, 0.0, xbcext_ref[tt:tt + HIST, :])

    def dt_stage():
        dt = _softplus(dtraw_ref[...] + dtb_ref[...])
        dt_ref[...] = dt
        a_ref[...] = dt * (-jnp.exp(alog_ref[...]))

    @pl.when(g == 0)
    def _():
        for r0 in row_blocks:
            norm_rows(x_ref, r0)
        for c0 in proj_blocks:
            project_cols(c0)
        xbcext_ref[0:HIST, :] = jnp.zeros((HIST, SSD_CONV_DIM), F32)
        for r0 in row_blocks:
            gate_rows(r0)
            ssd_conv_rows(r0)
        carry_xbc_history(g)
        dt_stage()

    @pl.when(g % tiles_per_seq == 0)
    def _():
        vext_ref[0:HIST, :] = jnp.zeros((HIST, CONV_WIDTH), F32)
        s_ref[...] = jnp.zeros(s_ref.shape, F32)

    def short_conv_rows(r0):
        acc = scw_ref[SHORT_K - 1:SHORT_K, :] * vext_ref[HIST + r0:HIST + r0 + rb, :]
        for k in range(1, SHORT_K):
            acc = acc + scw_ref[SHORT_K - 1 - k:SHORT_K - k, :] * vext_ref[HIST + r0 - k:HIST + r0 - k + rb, :]
        ymix_ref[r0:r0 + rb, 0:CONV_WIDTH] = (pj_ref[r0:r0 + rb, COL_UB:COL_UB + CONV_WIDTH] * acc).astype(BF16)

    row_i = lax.broadcasted_iota(jnp.int32, (2 * CHUNK, LANES), 0)
    col_i = lax.broadcasted_iota(jnp.int32, (2 * CHUNK, LANES), 1)
    same_half = (row_i >= CHUNK) == (col_i >= CHUNK)
    tril2 = jnp.where(same_half & (col_i <= row_i), 1.0, 0.0).astype(F32)
    sub_i = lax.broadcasted_iota(jnp.int32, (CHUNK, LANES), 0)
    lane_i = lax.broadcasted_iota(jnp.int32, (CHUNK, LANES), 1)
    causal2 = (lane_i & (CHUNK - 1)) <= sub_i
    left_half = lane_i < CHUNK

    def pair_cols(zmat, j):
        even = jnp.broadcast_to(zmat[0:CHUNK, j:j + 1], (CHUNK, LANES))
        odd = jnp.broadcast_to(zmat[CHUNK:2 * CHUNK, j:j + 1], (CHUNK, LANES))
        return jnp.where(left_half, even, odd)

    def chunk_prep(c):
        rows = slice(c * CHUNK, (c + 1) * CHUNK)
        a_z = jnp.concatenate([a_ref[rows, 0:LANES], a_ref[rows, LANES:2 * LANES]], axis=0)
        dt_z = jnp.concatenate([dt_ref[rows, 0:LANES], dt_ref[rows, LANES:2 * LANES]], axis=0)
        acum_z = jnp.dot(tril2, a_z, precision=lax.Precision.HIGHEST, preferred_element_type=F32)
        return rows, acum_z, dt_z, acum_z.T, dt_z.T

    def chunk_group(prep, grp):
        rows, acum_z, dt_z, acum_r, dt_r = prep
        pairs = [grp * PAIRS_PER_GROUP + jj for jj in range(PAIRS_PER_GROUP)]
        st = {}

        def scores():
            c_g = xbc_ref[rows, XBC_C + grp * SSD_STATE:XBC_C + (grp + 1) * SSD_STATE]
            b_g = xbc_ref[rows, XBC_B + grp * SSD_STATE:XBC_B + (grp + 1) * SSD_STATE]
            c_bf = c_g.astype(BF16)
            b2 = jnp.concatenate([b_g, b_g], axis=0)
            g2 = lax.dot_general(c_bf, b2.astype(BF16), (((1,), (1,)), ((), ())),
                                 preferred_element_type=F32)
            st["b_t"] = b2.T[:, 0:CHUNK].astype(BF16)
            st["s_g"] = s_ref[grp]
            st["cs"] = jnp.dot(c_bf, st["s_g"].astype(BF16), preferred_element_type=F32)
            st["col_a"] = [pair_cols(acum_z, j) for j in pairs]
            st["xs2"] = [xbc_ref[rows, j * LANES:(j + 1) * LANES] for j in pairs]
            st["m2"], st["xx"] = [], []
            for jj, j in enumerate(pairs):
                decay = jnp.exp(jnp.where(causal2, st["col_a"][jj] - acum_r[j:j + 1, :], -jnp.inf))
                st["m2"].append((g2 * decay * dt_r[j:j + 1, :]).astype(BF16))
                xs2 = st["xs2"][jj]
                st["xx"].append(jnp.where(same_half, jnp.concatenate([xs2, xs2], axis=0), 0.0).astype(BF16))

        def outputs():
            y_diag = [jnp.dot(st["m2"][jj], st["xx"][jj], preferred_element_type=F32)
                      for jj in range(PAIRS_PER_GROUP)]
            st["xd"], st["chunk_decay"] = [], []
            for jj, j in enumerate(pairs):
                cols = slice(j * LANES, (j + 1) * LANES)
                col_a, xs2 = st["col_a"][jj], st["xs2"][jj]
                y = y_diag[jj] + st["cs"][:, jj * LANES:(jj + 1) * LANES] * jnp.exp(col_a)
                y = y + dskip_ref[:, cols] * xs2
                ybuf_ref[rows, cols] = y * zs_ref[rows, cols]
                last_a = col_a[CHUNK - 1:CHUNK, :]
                st["xd"].append((xs2 * pair_cols(dt_z, j) * jnp.exp(last_a - col_a)).astype(BF16))
                st["chunk_decay"].append(jnp.exp(last_a))

        def state_update():
            ds = jnp.dot(st["b_t"], jnp.concatenate(st["xd"], axis=1), preferred_element_type=F32)
            s_ref[grp] = st["s_g"] * jnp.concatenate(st["chunk_decay"], axis=1) + ds

        return scores, outputs, state_update

    def group_norm_rows(r0):
        for grp in range(SSD_GROUPS):
            gc = slice(grp * GROUP_COLS, (grp + 1) * GROUP_COLS)
            yg = _rms(ybuf_ref[r0:r0 + rb, gc], gnw_ref[:, gc])
            ymix_ref[r0:r0 + rb, CONV_WIDTH + grp * GROUP_COLS:CONV_WIDTH + (grp + 1) * GROUP_COLS] = (
                yg.astype(BF16))

    pending = [c0 for c0 in proj_blocks if c0 >= COL_Z] + [c0 for c0 in proj_blocks if c0 < COL_Z]

    def project_next(n):
        for _ in range(n):
            if pending:
                project_cols(pending.pop(0))

    half = len(row_blocks) // 2
    for r0 in row_blocks:
        norm_rows(xn_ref, r0)
    project_next(1)
    for r0 in row_blocks:
        vext_ref[HIST + r0:HIST + r0 + rb, :] = (
            pj_ref[r0:r0 + rb, COL_UC:COL_UC + CONV_WIDTH] * pj_ref[r0:r0 + rb, COL_UH:COL_UH + CONV_WIDTH])
    project_next(1)
    for r0 in row_blocks[:half]:
        short_conv_rows(r0)
    project_next(1)
    for r0 in row_blocks[half:]:
        short_conv_rows(r0)
    vext_ref[0:HIST, :] = vext_ref[tt:tt + HIST, :]
    stages = []
    for c in range(tt // CHUNK):
        prep = chunk_prep(c)
        for grp in range(SSD_GROUPS):
            stages.extend(chunk_group(prep, grp))
    per_stage = len(pending) / len(stages)
    for i, stage in enumerate(stages):
        project_next(int((i + 1) * per_stage) - int(i * per_stage))
        stage()
    project_next(len(pending))
    for r0 in row_blocks:
        group_norm_rows(r0)
    ahead = [functools.partial(gate_rows, r0) for r0 in row_blocks]
    ahead += [functools.partial(ssd_conv_rows, r0) for r0 in row_blocks]
    out_blocks = list(range(0, D_MODEL, nc))
    per_block = -(-len(ahead) // len(out_blocks))
    for n0 in out_blocks:
        o_ref[:, n0:n0 + nc] = x_ref[:, n0:n0 + nc] + jnp.dot(
            ymix_ref[...], wout_ref[:, n0:n0 + nc], preferred_element_type=F32)
        for stage in ahead[:per_block]:
            stage()
        del ahead[:per_block]
    carry_xbc_history(g + 1)
    dt_stage()


def _mlp_kernel(x_ref, nw_ref, wup_ref, wdn_ref, fw_ref, win32_ref, wout32_ref, o_ref, win16_ref, wout16_ref,
                h_ref, hid_ref, *, tm, nc, rb, final_norm):
    win16_ref[...] = win32_ref[...].T.astype(BF16)
    wout16_ref[...] = wout32_ref[...].astype(BF16)
    for r0 in range(0, tm, rb):
        h_ref[r0:r0 + rb, :] = _rms(x_ref[r0:r0 + rb, :], nw_ref[...]).astype(BF16)
    for f0 in range(0, D_FF, nc):
        u = jnp.maximum(jnp.dot(h_ref[...], wup_ref[:, f0:f0 + nc], preferred_element_type=F32), 0.0)
        hid_ref[:, f0:f0 + nc] = (u * u).astype(BF16)
    for n0 in range(0, D_MODEL, nc):
        o_ref[:, n0:n0 + nc] = x_ref[:, n0:n0 + nc] + jnp.dot(
            hid_ref[...], wdn_ref[:, n0:n0 + nc], preferred_element_type=F32)
    if final_norm:
        for r0 in range(0, tm, rb):
            o_ref[r0:r0 + rb, :] = _rms(o_ref[r0:r0 + rb, :], fw_ref[...])


def _resident(shape, layer):
    nd = len(shape)
    return pl.BlockSpec((None,) + tuple(shape), lambda *_: (layer,) + (0,) * nd,
                        pipeline_mode=pl.Buffered(1))


def _whole(shape):
    return pl.BlockSpec(tuple(shape), lambda *_: (0,) * len(shape), pipeline_mode=pl.Buffered(1))


def _cast_specs(shape, steps, layer):
    rows, cols = shape
    n_blocks = 1
    while n_blocks * 2 <= min(steps, rows // BF16_ROWS):
        n_blocks *= 2
    assert rows % n_blocks == 0 and (rows // n_blocks) % BF16_ROWS == 0
    br = rows // n_blocks
    src = pl.BlockSpec((None, br, cols), lambda i: (layer, jnp.minimum(i, n_blocks - 1), 0))
    dst = pl.BlockSpec((br, cols), lambda i: (jnp.minimum(i, n_blocks - 1), 0))
    return src, dst


def _mixer_call(x2, p, w_main16, w_out16, w_up, w_down, layer, tt, tiles_per_seq, vmem_limit):
    n, d = x2.shape
    n_tiles = n // tt
    kern = functools.partial(_mixer_kernel, tt=tt, nc=512, pc=256, rb=32, tiles_per_seq=tiles_per_seq)
    in_specs = [
        pl.BlockSpec((tt, d), lambda g: (g, 0)),
        pl.BlockSpec((tt, d), lambda g: (jnp.minimum(g + 1, n_tiles - 1), 0)),
        _resident((1, d), layer),
        pl.BlockSpec((d, MAIN_COLS), lambda g: (0, 0), pipeline_mode=pl.Buffered(1)),
        _resident((d, 2 * LANES), layer),
        _resident((HIST, CONV_WIDTH), layer),
        _resident((HIST, SSD_CONV_DIM), layer),
        _resident((1, SSD_CONV_DIM), layer),
        _resident((1, 2 * LANES), layer),
        _resident((1, 2 * LANES), layer),
        _resident((1, SSD_INNER), layer),
        _resident((1, SSD_INNER), layer),
        _whole((MIX_WIDTH, d)),
    ]
    scratch = [
        pltpu.VMEM((tt, d), BF16),
        pltpu.VMEM((tt, COL_XBC), F32),
        pltpu.VMEM((tt + HIST, CONV_WIDTH), F32),
        pltpu.VMEM((tt + HIST, SSD_CONV_DIM), F32),
        pltpu.VMEM((tt, 2 * LANES), F32),
        pltpu.VMEM((tt, SSD_CONV_DIM), F32),
        pltpu.VMEM((tt, SSD_INNER), F32),
        pltpu.VMEM((tt, 2 * LANES), F32),
        pltpu.VMEM((tt, 2 * LANES), F32),
        pltpu.VMEM((tt, SSD_INNER), F32),
        pltpu.VMEM((tt, MIX_WIDTH), BF16),
        pltpu.VMEM((SSD_GROUPS, SSD_STATE, GROUP_COLS), F32),
    ]
    up_in, up_out = _cast_specs(w_up.shape[1:], n_tiles, layer)
    dn_in, dn_out = _cast_specs(w_down.shape[1:], n_tiles, layer)
    return pl.pallas_call(
        kern,
        out_shape=(jax.ShapeDtypeStruct(x2.shape, x2.dtype),
                   jax.ShapeDtypeStruct(w_up.shape[1:], BF16), jax.ShapeDtypeStruct(w_down.shape[1:], BF16)),
        grid=(n_tiles,),
        in_specs=in_specs + [up_in, dn_in],
        out_specs=(pl.BlockSpec((tt, d), lambda g: (g, 0)), up_out, dn_out),
        scratch_shapes=scratch,
        compiler_params=pltpu.CompilerParams(
            dimension_semantics=("arbitrary",), vmem_limit_bytes=vmem_limit),
        name=f"mixer_l{layer}",
    )(x2, x2, p["norm_mix_w"], w_main16, p["w_dt"], p["short_conv_w"], p["ssd_conv_w"], p["ssd_conv_b"],
      p["dt_bias"], p["a_log"], p["d_skip"], p["ssd_norm_w"], w_out16, w_up, w_down)


def _mlp_call(x2, p, w_up16, w_down16, w_in_t, w_out, layer, next_layer, tm, vmem_limit, final_norm):
    n, d = x2.shape
    steps = n // tm
    kern = functools.partial(_mlp_kernel, tm=tm, nc=512, rb=32, final_norm=final_norm)
    lane_tiles = MAIN_COLS // LANES
    n_blocks = max(k for k in range(1, lane_tiles + 1) if lane_tiles % k == 0 and k <= steps)
    bc = MAIN_COLS // n_blocks
    in_in = pl.BlockSpec((None, bc, d), lambda i: (next_layer, jnp.minimum(i, n_blocks - 1), 0))
    in_out = pl.BlockSpec((d, bc), lambda i: (0, jnp.minimum(i, n_blocks - 1)))
    out_in, out_out = _cast_specs(w_out.shape[1:], steps, next_layer)
    return pl.pallas_call(
        kern,
        out_shape=(jax.ShapeDtypeStruct(x2.shape, x2.dtype),
                   jax.ShapeDtypeStruct((d, MAIN_COLS), BF16), jax.ShapeDtypeStruct(w_out.shape[1:], BF16)),
        grid=(steps,),
        in_specs=[
            pl.BlockSpec((tm, d), lambda i: (i, 0)),
            _resident((1, d), layer),
            _whole((d, D_FF)),
            _whole((D_FF, d)),
            pl.BlockSpec((1, d), lambda i: (0, 0)),
            in_in, out_in,
        ],
        out_specs=(pl.BlockSpec((tm, d), lambda i: (i, 0)), in_out, out_out),
        scratch_shapes=[pltpu.VMEM((tm, d), BF16), pltpu.VMEM((tm, D_FF), BF16)],
        compiler_params=pltpu.CompilerParams(
            dimension_semantics=("arbitrary",), vmem_limit_bytes=vmem_limit),
        name=f"mlp_l{layer}",
    )(x2, p["norm_mlp_w"], w_up16, w_down16, p["final_norm_w"], w_in_t, w_out)


def _even_odd_lanes(v):
    pad = [(0, 0)] * (v.ndim - 1) + [(0, LANES - PAIRS)]
    return jnp.concatenate([jnp.pad(v[..., 0::2], pad), jnp.pad(v[..., 1::2], pad)], axis=-1)


def _pad_rows(w, rows):
    return jnp.pad(w, ((0, 0), (0, rows - w.shape[1]), (0, 0)))


def kernel(x, norm_mix_w, w_in, short_conv_w, ssd_conv_w, ssd_conv_b, dt_bias, a_log, d_skip, ssd_norm_w,
           w_out, norm_mlp_w, w_up, w_down, final_norm_w):
    depth = w_in.shape[0]
    b, t, d = x.shape
    tt = min(256, t)
    tm = min(1024, b * t)
    assert d == D_MODEL and t % tt == 0 and tt % CHUNK == 0 and (b * t) % tm == 0
    vmem_limit = 56 * 1024 * 1024
    p = {
        "norm_mix_w": norm_mix_w[:, None, :],
        "w_dt": _even_odd_lanes(w_in[:, :, MAIN_COLS:]).astype(BF16),
        "short_conv_w": _pad_rows(short_conv_w, HIST),
        "ssd_conv_w": _pad_rows(ssd_conv_w, HIST),
        "ssd_conv_b": ssd_conv_b[:, None, :],
        "dt_bias": _even_odd_lanes(dt_bias)[:, None, :],
        "a_log": _even_odd_lanes(a_log)[:, None, :],
        "d_skip": jnp.repeat(d_skip, SSD_HEAD_DIM, axis=-1)[:, None, :],
        "ssd_norm_w": ssd_norm_w[:, None, :],
        "norm_mlp_w": norm_mlp_w[:, None, :],
        "final_norm_w": final_norm_w[None, :],
    }
    x2 = x.reshape(b * t, d)
    w_main16, w_out16 = w_in[0, :, :MAIN_COLS].astype(BF16), w_out[0].astype(BF16)
    w_in_t = jnp.swapaxes(w_in, 1, 2)
    for layer in range(depth):
        x2, w_up16, w_down16 = _mixer_call(x2, p, w_main16, w_out16, w_up, w_down, layer, tt, t // tt, vmem_limit)
        x2, w_main16, w_out16 = _mlp_call(x2, p, w_up16, w_down16, w_in_t, w_out, layer, (layer + 1) % depth, tm,
                                          vmem_limit, layer == depth - 1)
    return x2.reshape(b, t, d)
```

```python
import functools

import jax
import jax.numpy as jnp
from jax import lax
from jax.experimental import pallas as pl
from jax.experimental.pallas import tpu as pltpu

D_MODEL = 1024
CHUNK = 64
EPS = 1e-5
CONV_WIDTH = 1024
SHORT_K = 3
SSD_HEAD_DIM = 64
SSD_HEADS = 16
SSD_INNER = 1024
SSD_GROUPS = 2
SSD_STATE = 128
SSD_CONV_K = 4
SSD_CONV_DIM = SSD_INNER + 2 * SSD_GROUPS * SSD_STATE
MIX_WIDTH = CONV_WIDTH + SSD_INNER
D_FF = 4 * D_MODEL
MAIN_COLS = 3 * CONV_WIDTH + SSD_INNER + SSD_CONV_DIM
COL_UB, COL_UC, COL_UH, COL_Z, COL_XBC = 0, 1024, 2048, 3072, 4096
XBC_B = SSD_INNER
XBC_C = SSD_INNER + SSD_GROUPS * SSD_STATE
LANES = 128
HIST = 8
BF16_ROWS = 16
PAIRS = SSD_HEADS // 2
PAIRS_PER_GROUP = PAIRS // SSD_GROUPS
GROUP_COLS = SSD_INNER // SSD_GROUPS

F32 = jnp.float32
BF16 = jnp.bfloat16


def _rms(x, w):
    ms = jnp.mean(x * x, axis=-1, keepdims=True)
    return x * lax.rsqrt(ms + EPS) * w


def _softplus(x):
    return jnp.maximum(x, 0.0) + jnp.log1p(jnp.exp(-jnp.abs(x)))


def _silu(x):
    return x * jax.nn.sigmoid(x)


def _mixer_kernel(x_ref, xn_ref, nw_ref, wmain_ref, wdt_ref, scw_ref, ssw_ref, ssb_ref, dtb_ref, alog_ref,
                  dskip_ref, gnw_ref, wout_ref, wup32_ref, wdn32_ref, o_ref, wup16_ref, wdn16_ref,
                  h_ref, pj_ref, vext_ref, xbcext_ref, dtraw_ref, xbc_ref, zs_ref, dt_ref, a_ref, ybuf_ref,
                  ymix_ref, s_ref, *, tt, nc, rb, tiles_per_seq):
    g = pl.program_id(0)

    wup16_ref[...] = wup32_ref[...].astype(BF16)
    wdn16_ref[...] = wdn32_ref[...].astype(BF16)

    def norm_rows(src_ref, r0):
        h_ref[r0:r0 + rb, :] = _rms(src_ref[r0:r0 + rb, :], nw_ref[...]).astype(BF16)

    def project_cols(c0):
        if c0 < COL_XBC:
            pj_ref[:, c0:c0 + nc] = jnp.dot(h_ref[...], wmain_ref[:, c0:c0 + nc], preferred_element_type=F32)
        elif c0 < MAIN_COLS:
            xbcext_ref[HIST:HIST + tt, c0 - COL_XBC:c0 - COL_XBC + nc] = jnp.dot(
                h_ref[...], wmain_ref[:, c0:c0 + nc], preferred_element_type=F32)
        else:
            dtraw_ref[...] = jnp.dot(h_ref[...], wdt_ref[...], preferred_element_type=F32)

    proj_blocks = list(range(0, MAIN_COLS, nc)) + [MAIN_COLS]
    row_blocks = list(range(0, tt, rb))

    def gate_rows(r0):
        zs_ref[r0:r0 + rb, :] = _silu(pj_ref[r0:r0 + rb, COL_Z:COL_Z + SSD_INNER])

    def ssd_conv_rows(r0):
        acc = ssb_ref[...] + ssw_ref[SSD_CONV_K - 1:SSD_CONV_K, :] * xbcext_ref[HIST + r0:HIST + r0 + rb, :]
        for k in range(1, SSD_CONV_K):
            acc = acc + (ssw_ref[SSD_CONV_K - 1 - k:SSD_CONV_K - k, :]
                         * xbcext_ref[HIST + r0 - k:HIST + r0 - k + rb, :])
        xbc_ref[r0:r0 + rb, :] = _silu(acc)

    def carry_xbc_history(tile):
        next_starts_seq = (tile + 1) % tiles_per_seq == 0
        xbcext_ref[0:HIST, :] = jnp.where(next_starts_seq, 0.0, xbcext_ref[tt:tt + HIST, :])

    def dt_stage():
        dt = _softplus(dtraw_ref[...] + dtb_ref[...])
        dt_ref[...] = dt
        a_ref[...] = dt * (-jnp.exp(alog_ref[...]))

    @pl.when(g == 0)
    def _():
        for r0 in row_blocks:
            norm_rows(x_ref, r0)
        for c0 in proj_blocks:
            project_cols(c0)
        xbcext_ref[0:HIST, :] = jnp.zeros((HIST, SSD_CONV_DIM), F32)
        for r0 in row_blocks:
            gate_rows(r0)
            ssd_conv_rows(r0)
        carry_xbc_history(g)
        dt_stage()

    @pl.when(g % tiles_per_seq == 0)
    def _():
        vext_ref[0:HIST, :] = jnp.zeros((HIST, CONV_WIDTH), F32)
        s_ref[...] = jnp.zeros(s_ref.shape, F32)

    def short_conv_rows(r0):
        acc = scw_ref[SHORT_K - 1:SHORT_K, :] * vext_ref[HIST + r0:HIST + r0 + rb, :]
        for k in range(1, SHORT_K):
            acc = acc + scw_ref[SHORT_K - 1 - k:SHORT_K - k, :] * vext_ref[HIST + r0 - k:HIST + r0 - k + rb, :]
        ymix_ref[r0:r0 + rb, 0:CONV_WIDTH] = (pj_ref[r0:r0 + rb, COL_UB:COL_UB + CONV_WIDTH] * acc).astype(BF16)

    row_i = lax.broadcasted_iota(jnp.int32, (2 * CHUNK, LANES), 0)
    col_i = lax.broadcasted_iota(jnp.int32, (2 * CHUNK, LANES), 1)
    same_half = (row_i >= CHUNK) == (col_i >= CHUNK)
    tril2 = jnp.where(same_half & (col_i <= row_i), 1.0, 0.0).astype(F32)
    sub_i = lax.broadcasted_iota(jnp.int32, (CHUNK, LANES), 0)
    lane_i = lax.broadcasted_iota(jnp.int32, (CHUNK, LANES), 1)
    causal2 = (lane_i & (CHUNK - 1)) <= sub_i
    left_half = lane_i < CHUNK

    def pair_cols(zmat, j):
        even = jnp.broadcast_to(zmat[0:CHUNK, j:j + 1], (CHUNK, LANES))
        odd = jnp.broadcast_to(zmat[CHUNK:2 * CHUNK, j:j + 1], (CHUNK, LANES))
        return jnp.where(left_half, even, odd)

    def chunk_prep(c):
        rows = slice(c * CHUNK, (c + 1) * CHUNK)
        a_z = jnp.concatenate([a_ref[rows, 0:LANES], a_ref[rows, LANES:2 * LANES]], axis=0)
        dt_z = jnp.concatenate([dt_ref[rows, 0:LANES], dt_ref[rows, LANES:2 * LANES]], axis=0)
        acum_z = jnp.dot(tril2, a_z, precision=lax.Precision.HIGHEST, preferred_element_type=F32)
        return rows, acum_z, dt_z, acum_z.T, dt_z.T

    def chunk_group(prep, grp):
        rows, acum_z, dt_z, acum_r, dt_r = prep
        c_g = xbc_ref[rows, XBC_C + grp * SSD_STATE:XBC_C + (grp + 1) * SSD_STATE]
        b_g = xbc_ref[rows, XBC_B + grp * SSD_STATE:XBC_B + (grp + 1) * SSD_STATE]
        c_bf = c_g.astype(BF16)
        b2 = jnp.concatenate([b_g, b_g], axis=0)
        g2 = lax.dot_general(c_bf, b2.astype(BF16), (((1,), (1,)), ((), ())),
                             preferred_element_type=F32)
        b_t = b2.T[:, 0:CHUNK].astype(BF16)
        s_g = s_ref[grp]
        cs = jnp.dot(c_bf, s_g.astype(BF16), preferred_element_type=F32)
        pairs = [grp * PAIRS_PER_GROUP + jj for jj in range(PAIRS_PER_GROUP)]
        col_a = [pair_cols(acum_z, j) for j in pairs]
        col_dt = [pair_cols(dt_z, j) for j in pairs]
        xs2 = [xbc_ref[rows, j * LANES:(j + 1) * LANES] for j in pairs]
        y_diag = []
        for jj, j in enumerate(pairs):
            decay = jnp.exp(jnp.where(causal2, col_a[jj] - acum_r[j:j + 1, :], -jnp.inf))
            m2 = (g2 * decay * dt_r[j:j + 1, :]).astype(BF16)
            xs_bf = xs2[jj].astype(BF16)
            xx = jnp.where(same_half, jnp.concatenate([xs_bf, xs_bf], axis=0), jnp.zeros((), BF16))
            y_diag.append(jnp.dot(m2, xx, preferred_element_type=F32))
        xd, chunk_decay = [], []
        for jj, j in enumerate(pairs):
            cols = slice(j * LANES, (j + 1) * LANES)
            y = y_diag[jj] + cs[:, jj * LANES:(jj + 1) * LANES] * jnp.exp(col_a[jj])
            y = y + dskip_ref[:, cols] * xs2[jj]
            ybuf_ref[rows, cols] = y * zs_ref[rows, cols]
            last_a = col_a[jj][CHUNK - 1:CHUNK, :]
            xd.append((xs2[jj] * col_dt[jj] * jnp.exp(last_a - col_a[jj])).astype(BF16))
            chunk_decay.append(jnp.exp(last_a))
        ds = jnp.dot(b_t, jnp.concatenate(xd, axis=1), preferred_element_type=F32)
        s_ref[grp] = s_g * jnp.concatenate(chunk_decay, axis=1) + ds

    def group_norm_rows(r0):
        for grp in range(SSD_GROUPS):
            gc = slice(grp * GROUP_COLS, (grp + 1) * GROUP_COLS)
            yg = _rms(ybuf_ref[r0:r0 + rb, gc], gnw_ref[:, gc])
            ymix_ref[r0:r0 + rb, CONV_WIDTH + grp * GROUP_COLS:CONV_WIDTH + (grp + 1) * GROUP_COLS] = (
                yg.astype(BF16))

    pending = [c0 for c0 in proj_blocks if c0 >= COL_Z] + [c0 for c0 in proj_blocks if c0 < COL_Z]

    def project_next(n):
        for _ in range(n):
            if pending:
                project_cols(pending.pop(0))

    half = len(row_blocks) // 2
    for r0 in row_blocks:
        norm_rows(xn_ref, r0)
    project_next(1)
    for r0 in row_blocks:
        vext_ref[HIST + r0:HIST + r0 + rb, :] = (
            pj_ref[r0:r0 + rb, COL_UC:COL_UC + CONV_WIDTH] * pj_ref[r0:r0 + rb, COL_UH:COL_UH + CONV_WIDTH])
    project_next(1)
    for r0 in row_blocks[:half]:
        short_conv_rows(r0)
    project_next(1)
    for r0 in row_blocks[half:]:
        short_conv_rows(r0)
    vext_ref[0:HIST, :] = vext_ref[tt:tt + HIST, :]
    n_chunks = tt // CHUNK
    for c in range(n_chunks):
        prep = chunk_prep(c)
        for grp in range(SSD_GROUPS):
            project_next(1)
            chunk_group(prep, grp)
    project_next(len(pending))
    for r0 in row_blocks:
        group_norm_rows(r0)
    ahead = [functools.partial(gate_rows, r0) for r0 in row_blocks]
    ahead += [functools.partial(ssd_conv_rows, r0) for r0 in row_blocks]
    out_blocks = list(range(0, D_MODEL, nc))
    per_block = -(-len(ahead) // len(out_blocks))
    for n0 in out_blocks:
        o_ref[:, n0:n0 + nc] = x_ref[:, n0:n0 + nc] + jnp.dot(
            ymix_ref[...], wout_ref[:, n0:n0 + nc], preferred_element_type=F32)
        for stage in ahead[:per_block]:
            stage()
        del ahead[:per_block]
    carry_xbc_history(g + 1)
    dt_stage()


def _mlp_kernel(x_ref, nw_ref, wup_ref, wdn_ref, fw_ref, win32_ref, wout32_ref, o_ref, win16_ref, wout16_ref,
                h_ref, hid_ref, *, tm, nc, rb, final_norm):
    win16_ref[...] = win32_ref[...].T.astype(BF16)
    wout16_ref[...] = wout32_ref[...].astype(BF16)
    for r0 in range(0, tm, rb):
        h_ref[r0:r0 + rb, :] = _rms(x_ref[r0:r0 + rb, :], nw_ref[...]).astype(BF16)
    for f0 in range(0, D_FF, nc):
        u = jnp.maximum(jnp.dot(h_ref[...], wup_ref[:, f0:f0 + nc], preferred_element_type=F32), 0.0)
        hid_ref[:, f0:f0 + nc] = (u * u).astype(BF16)
    for n0 in range(0, D_MODEL, nc):
        o_ref[:, n0:n0 + nc] = x_ref[:, n0:n0 + nc] + jnp.dot(
            hid_ref[...], wdn_ref[:, n0:n0 + nc], preferred_element_type=F32)
    if final_norm:
        for r0 in range(0, tm, rb):
            o_ref[r0:r0 + rb, :] = _rms(o_ref[r0:r0 + rb, :], fw_ref[...])


def _resident(shape, layer):
    nd = len(shape)
    return pl.BlockSpec((None,) + tuple(shape), lambda *_: (layer,) + (0,) * nd,
                        pipeline_mode=pl.Buffered(1))


def _whole(shape):
    return pl.BlockSpec(tuple(shape), lambda *_: (0,) * len(shape), pipeline_mode=pl.Buffered(1))


def _cast_specs(shape, steps, layer):
    rows, cols = shape
    n_blocks = 1
    while n_blocks * 2 <= min(steps, rows // BF16_ROWS):
        n_blocks *= 2
    assert rows % n_blocks == 0 and (rows // n_blocks) % BF16_ROWS == 0
    br = rows // n_blocks
    src = pl.BlockSpec((None, br, cols), lambda i: (layer, jnp.minimum(i, n_blocks - 1), 0))
    dst = pl.BlockSpec((br, cols), lambda i: (jnp.minimum(i, n_blocks - 1), 0))
    return src, dst


def _mixer_call(x2, p, w_main16, w_out16, w_up, w_down, layer, tt, tiles_per_seq, vmem_limit):
    n, d = x2.shape
    n_tiles = n // tt
    kern = functools.partial(_mixer_kernel, tt=tt, nc=512, rb=32, tiles_per_seq=tiles_per_seq)
    in_specs = [
        pl.BlockSpec((tt, d), lambda g: (g, 0)),
        pl.BlockSpec((tt, d), lambda g: (jnp.minimum(g + 1, n_tiles - 1), 0)),
        _resident((1, d), layer),
        pl.BlockSpec((d, MAIN_COLS), lambda g: (0, 0), pipeline_mode=pl.Buffered(1)),
        _resident((d, 2 * LANES), layer),
        _resident((HIST, CONV_WIDTH), layer),
        _resident((HIST, SSD_CONV_DIM), layer),
        _resident((1, SSD_CONV_DIM), layer),
        _resident((1, 2 * LANES), layer),
        _resident((1, 2 * LANES), layer),
        _resident((1, SSD_INNER), layer),
        _resident((1, SSD_INNER), layer),
        _whole((MIX_WIDTH, d)),
    ]
    scratch = [
        pltpu.VMEM((tt, d), BF16),
        pltpu.VMEM((tt, COL_XBC), F32),
        pltpu.VMEM((tt + HIST, CONV_WIDTH), F32),
        pltpu.VMEM((tt + HIST, SSD_CONV_DIM), F32),
        pltpu.VMEM((tt, 2 * LANES), F32),
        pltpu.VMEM((tt, SSD_CONV_DIM), F32),
        pltpu.VMEM((tt, SSD_INNER), F32),
        pltpu.VMEM((tt, 2 * LANES), F32),
        pltpu.VMEM((tt, 2 * LANES), F32),
        pltpu.VMEM((tt, SSD_INNER), F32),
        pltpu.VMEM((tt, MIX_WIDTH), BF16),
        pltpu.VMEM((SSD_GROUPS, SSD_STATE, GROUP_COLS), F32),
    ]
    up_in, up_out = _cast_specs(w_up.shape[1:], n_tiles, layer)
    dn_in, dn_out = _cast_specs(w_down.shape[1:], n_tiles, layer)
    return pl.pallas_call(
        kern,
        out_shape=(jax.ShapeDtypeStruct(x2.shape, x2.dtype),
                   jax.ShapeDtypeStruct(w_up.shape[1:], BF16), jax.ShapeDtypeStruct(w_down.shape[1:], BF16)),
        grid=(n_tiles,),
        in_specs=in_specs + [up_in, dn_in],
        out_specs=(pl.BlockSpec((tt, d), lambda g: (g, 0)), up_out, dn_out),
        scratch_shapes=scratch,
        compiler_params=pltpu.CompilerParams(
            dimension_semantics=("arbitrary",), vmem_limit_bytes=vmem_limit),
        name=f"mixer_l{layer}",
    )(x2, x2, p["norm_mix_w"], w_main16, p["w_dt"], p["short_conv_w"], p["ssd_conv_w"], p["ssd_conv_b"],
      p["dt_bias"], p["a_log"], p["d_skip"], p["ssd_norm_w"], w_out16, w_up, w_down)


def _mlp_call(x2, p, w_up16, w_down16, w_in_t, w_out, layer, next_layer, tm, vmem_limit, final_norm):
    n, d = x2.shape
    steps = n // tm
    kern = functools.partial(_mlp_kernel, tm=tm, nc=512, rb=32, final_norm=final_norm)
    lane_tiles = MAIN_COLS // LANES
    n_blocks = max(k for k in range(1, lane_tiles + 1) if lane_tiles % k == 0 and k <= steps)
    bc = MAIN_COLS // n_blocks
    in_in = pl.BlockSpec((None, bc, d), lambda i: (next_layer, jnp.minimum(i, n_blocks - 1), 0))
    in_out = pl.BlockSpec((d, bc), lambda i: (0, jnp.minimum(i, n_blocks - 1)))
    out_in, out_out = _cast_specs(w_out.shape[1:], steps, next_layer)
    return pl.pallas_call(
        kern,
        out_shape=(jax.ShapeDtypeStruct(x2.shape, x2.dtype),
                   jax.ShapeDtypeStruct((d, MAIN_COLS), BF16), jax.ShapeDtypeStruct(w_out.shape[1:], BF16)),
        grid=(steps,),
        in_specs=[
            pl.BlockSpec((tm, d), lambda i: (i, 0)),
            _resident((1, d), layer),
            _whole((d, D_FF)),
            _whole((D_FF, d)),
            pl.BlockSpec((1, d), lambda i: (0, 0)),
            in_in, out_in,
        ],
        out_specs=(pl.BlockSpec((tm, d), lambda i: (i, 0)), in_out, out_out),
        scratch_shapes=[pltpu.VMEM((tm, d), BF16), pltpu.VMEM((tm, D_FF), BF16)],
        compiler_params=pltpu.CompilerParams(
            dimension_semantics=("arbitrary",), vmem_limit_bytes=vmem_limit),
        name=f"mlp_l{layer}",
    )(x2, p["norm_mlp_w"], w_up16, w_down16, p["final_norm_w"], w_in_t, w_out)


def _even_odd_lanes(v):
    pad = [(0, 0)] * (v.ndim - 1) + [(0, LANES - PAIRS)]
    return jnp.concatenate([jnp.pad(v[..., 0::2], pad), jnp.pad(v[..., 1::2], pad)], axis=-1)


def _pad_rows(w, rows):
    return jnp.pad(w, ((0, 0), (0, rows - w.shape[1]), (0, 0)))


def kernel(x, norm_mix_w, w_in, short_conv_w, ssd_conv_w, ssd_conv_b, dt_bias, a_log, d_skip, ssd_norm_w,
           w_out, norm_mlp_w, w_up, w_down, final_norm_w):
    depth = w_in.shape[0]
    b, t, d = x.shape
    tt = min(256, t)
    tm = min(1024, b * t)
    assert d == D_MODEL and t % tt == 0 and tt % CHUNK == 0 and (b * t) % tm == 0
    vmem_limit = 56 * 1024 * 1024
    p = {
        "norm_mix_w": norm_mix_w[:, None, :],
        "w_dt": _even_odd_lanes(w_in[:, :, MAIN_COLS:]).astype(BF16),
        "short_conv_w": _pad_rows(short_conv_w, HIST),
        "ssd_conv_w": _pad_rows(ssd_conv_w, HIST),
        "ssd_conv_b": ssd_conv_b[:, None, :],
        "dt_bias": _even_odd_lanes(dt_bias)[:, None, :],
        "a_log": _even_odd_lanes(a_log)[:, None, :],
        "d_skip": jnp.repeat(d_skip, SSD_HEAD_DIM, axis=-1)[:, None, :],
        "ssd_norm_w": ssd_norm_w[:, None, :],
        "norm_mlp_w": norm_mlp_w[:, None, :],
        "final_norm_w": final_norm_w[None, :],
    }
    x2 = x.reshape(b * t, d)
    w_main16, w_out16 = w_in[0, :, :MAIN_COLS].astype(BF16), w_out[0].astype(BF16)
    w_in_t = jnp.swapaxes(w_in, 1, 2)
    for layer in range(depth):
        x2, w_up16, w_down16 = _mixer_call(x2, p, w_main16, w_out16, w_up, w_down, layer, tt, t // tt, vmem_limit)
        x2, w_main16, w_out16 = _mlp_call(x2, p, w_up16, w_down16, w_in_t, w_out, layer, (layer + 1) % depth, tm,
                                          vmem_limit, layer == depth - 1)
    return x2.reshape(b, t, d)
```
